```python
import jax, jax.numpy as jnp
from jax import lax
import numpy as np

D_MODEL = 1024
BATCH = 32
SEQ = 2048
DEPTH = 4
DEC_BATCH = 8
DEC_SEQ = 16
PAST_LEN = 2048

CHUNK = 64
MIX_WIDTH = D_MODEL
CONV_CH = 512
CONV_K = 31
POOL_CH = MIX_WIDTH - CONV_CH
POOL_WINDOWS = (2, 4, 8, 16)
POOL_GROUPS = 4
POOL_GC = POOL_CH // POOL_GROUPS
POOL_HIST = 15
D_FF = 2816
N_EXPERTS = 8
TOP_K = 2
EXPERT_FF = 2816
N_DENSE = (DEPTH + 1) // 2
N_MOE = DEPTH // 2
EPS = 1e-6

kernel_name = 'hybrid_conv_pool_stream_step'


def rms_norm(x, g):
    xf = x.astype(jnp.float32)
    y = xf * lax.rsqrt(jnp.mean(xf * xf, axis=-1, keepdims=True) + EPS)
    return (y * g.astype(jnp.float32)).astype(x.dtype)


def layer_norm(x, g, b):
    xf = x.astype(jnp.float32)
    mu = jnp.mean(xf, axis=-1, keepdims=True)
    xc = xf - mu
    y = xc * lax.rsqrt(jnp.mean(xc * xc, axis=-1, keepdims=True) + EPS)
    return (y * g.astype(jnp.float32) + b.astype(jnp.float32)).astype(x.dtype)


def conv_group(a_hist, a_new, conv_w, conv_b, ln_g, ln_b):
    ext = jnp.concatenate([a_hist, a_new], axis=1)
    y = lax.conv_general_dilated(ext, conv_w[:, None, :], window_strides=(1,), padding='VALID',
                                 dimension_numbers=('NWC', 'WIO', 'NWC'),
                                 feature_group_count=CONV_CH)
    y = jax.nn.silu(layer_norm(y + conv_b, ln_g, ln_b))
    return y, ext[:, -(CONV_K - 1):]


def pool_group(p_hist, p_new, pos0, pool_w, pool_scale):
    b, l, _ = p_new.shape
    ext = jnp.concatenate([p_hist, p_new], axis=1)
    cs = jnp.cumsum(ext.astype(jnp.float32), axis=1)
    cs = jnp.concatenate([jnp.zeros((b, 1, POOL_CH), jnp.float32), cs], axis=1)
    pos1 = jnp.arange(l, dtype=jnp.float32) + jnp.float32(pos0 + 1)
    means = []
    for g, w in enumerate(POOL_WINDOWS):
        sl = slice(g * POOL_GC, (g + 1) * POOL_GC)
        s = cs[:, POOL_HIST + 1:POOL_HIST + 1 + l, sl] - cs[:, POOL_HIST + 1 - w:POOL_HIST + 1 - w + l, sl]
        cnt = jnp.minimum(pos1, jnp.float32(w))[None, :, None]
        means.append(s / cnt)
    d = (jnp.concatenate(means, axis=-1) - p_new.astype(jnp.float32)).astype(p_new.dtype)
    z = jnp.einsum('blgc,gcd->blgd', d.reshape(b, l, POOL_GROUPS, POOL_GC), pool_w)
    return z.reshape(b, l, POOL_CH) * pool_scale, ext[:, -POOL_HIST:]


def mixer_layer(x, conv_hist, pool_hist, pos0, norm_g, w_in, conv_w, conv_b, ln_g, ln_b, pool_w, pool_scale, w_out):
    h = rms_norm(x, norm_g)
    u = jnp.einsum('bld,de->ble', h, w_in)
    a = u[..., :CONV_CH] * jax.nn.sigmoid(u[..., CONV_CH:2 * CONV_CH])
    c, new_conv = conv_group(conv_hist, a, conv_w, conv_b, ln_g, ln_b)
    z, new_pool = pool_group(pool_hist, u[..., 2 * CONV_CH:], pos0, pool_w, pool_scale)
    mix = jnp.einsum('ble,ed->bld', jnp.concatenate([c, z], axis=-1), w_out)
    return x + mix, new_conv, new_pool


def swiglu(h, wg, wu, wd):
    return jnp.einsum('blf,fd->bld', jax.nn.silu(jnp.einsum('bld,df->blf', h, wg)) * jnp.einsum('bld,df->blf', h, wu), wd)


def moe_swiglu(h, router_w, router_b, wg, wu, wd):
    logits = jnp.einsum('bld,de->ble', h, router_w).astype(jnp.float32) + router_b.astype(jnp.float32)
    top_v, top_i = lax.top_k(logits, TOP_K)
    top_p = jax.nn.softmax(top_v, axis=-1)
    gates = jnp.sum(jax.nn.one_hot(top_i, N_EXPERTS, dtype=jnp.float32) * top_p[..., None], axis=-2)
    out = jnp.zeros_like(h)
    for e in range(N_EXPERTS):
        out = out + gates[..., e:e + 1].astype(h.dtype) * swiglu(h, wg[e], wu[e], wd[e])
    return out


def trunk(x, conv_hist, pool_hist, pos0, weights):
    (norm_mix, w_in, conv_w, conv_b, conv_ln_g, conv_ln_b, pool_w, pool_scale, w_out, norm_ffn,
     dense_w_gate, dense_w_up, dense_w_down, router_w, router_b, moe_w_gate, moe_w_up, moe_w_down, norm_final) = weights
    new_conv, new_pool = [], []
    for l in range(DEPTH):
        x, nc, npl = mixer_layer(x, conv_hist[l], pool_hist[l], pos0, norm_mix[l], w_in[l], conv_w[l], conv_b[l],
                                 conv_ln_g[l], conv_ln_b[l], pool_w[l], pool_scale[l], w_out[l])
        new_conv.append(nc)
        new_pool.append(npl)
        h = rms_norm(x, norm_ffn[l])
        j = l // 2
        if l % 2 == 0:
            x = x + swiglu(h, dense_w_gate[j], dense_w_up[j], dense_w_down[j])
        else:
            x = x + moe_swiglu(h, router_w[j], router_b[j], moe_w_gate[j], moe_w_up[j], moe_w_down[j])
    return rms_norm(x, norm_final), jnp.stack(new_conv), jnp.stack(new_pool)


def setup_inputs(seed: int = 0) -> dict:
    key = jax.random.key(seed)
    ks = jax.random.split(key, 24)
    f32 = jnp.float32

    def nrm(k, shape, s):
        return s * jax.random.normal(k, shape, f32)

    return {
        'x_prompt': nrm(ks[0], (BATCH, SEQ, D_MODEL), 1.0),
        'x_sample': nrm(ks[1], (DEC_BATCH, DEC_SEQ, D_MODEL), 1.0),
        'state_conv': nrm(ks[2], (DEPTH, DEC_BATCH, CONV_K - 1, CONV_CH), 0.5),
        'state_pool': nrm(ks[3], (DEPTH, DEC_BATCH, POOL_HIST, POOL_CH), 1.0),
        'norm_mix': 1.0 + nrm(ks[4], (DEPTH, D_MODEL), 0.05),
        'w_in': nrm(ks[5], (DEPTH, D_MODEL, 2 * CONV_CH + POOL_CH), D_MODEL ** -0.5),
        'conv_w': nrm(ks[6], (DEPTH, CONV_K, CONV_CH), CONV_K ** -0.5),
        'conv_b': nrm(ks[7], (DEPTH, CONV_CH), 0.02),
        'conv_ln_g': 1.0 + nrm(ks[8], (DEPTH, CONV_CH), 0.05),
        'conv_ln_b': nrm(ks[9], (DEPTH, CONV_CH), 0.02),
        'pool_w': nrm(ks[10], (DEPTH, POOL_GROUPS, POOL_GC, POOL_GC), POOL_GC ** -0.5),
        'pool_scale': 1.0 + nrm(ks[11], (DEPTH, POOL_CH), 0.1),
        'w_out': nrm(ks[12], (DEPTH, MIX_WIDTH, D_MODEL), MIX_WIDTH ** -0.5),
        'norm_ffn': 1.0 + nrm(ks[13], (DEPTH, D_MODEL), 0.05),
        'dense_w_gate': nrm(ks[14], (N_DENSE, D_MODEL, D_FF), D_MODEL ** -0.5),
        'dense_w_up': nrm(ks[15], (N_DENSE, D_MODEL, D_FF), D_MODEL ** -0.5),
        'dense_w_down': nrm(ks[16], (N_DENSE, D_FF, D_MODEL), D_FF ** -0.5),
        'router_w': nrm(ks[17], (N_MOE, D_MODEL, N_EXPERTS), D_MODEL ** -0.5),
        'router_b': nrm(ks[18], (N_MOE, N_EXPERTS), 0.01),
        'moe_w_gate': nrm(ks[19], (N_MOE, N_EXPERTS, D_MODEL, EXPERT_FF), D_MODEL ** -0.5),
        'moe_w_up': nrm(ks[20], (N_MOE, N_EXPERTS, D_MODEL, EXPERT_FF), D_MODEL ** -0.5),
        'moe_w_down': nrm(ks[21], (N_MOE, N_EXPERTS, EXPERT_FF, D_MODEL), EXPERT_FF ** -0.5),
        'norm_final': 1.0 + nrm(ks[22], (D_MODEL,), 0.05),
    }


def reference(x_prompt, x_sample, state_conv, state_pool, norm_mix, w_in, conv_w, conv_b, conv_ln_g, conv_ln_b,
              pool_w, pool_scale, w_out, norm_ffn, dense_w_gate, dense_w_up, dense_w_down, router_w, router_b,
              moe_w_gate, moe_w_up, moe_w_down, norm_final):
    weights = (norm_mix, w_in, conv_w, conv_b, conv_ln_g, conv_ln_b, pool_w, pool_scale, w_out, norm_ffn,
               dense_w_gate, dense_w_up, dense_w_down, router_w, router_b, moe_w_gate, moe_w_up, moe_w_down, norm_final)
    bp = x_prompt.shape[0]
    zero_conv = jnp.zeros((DEPTH, bp, CONV_K - 1, CONV_CH), x_prompt.dtype)
    zero_pool = jnp.zeros((DEPTH, bp, POOL_HIST, POOL_CH), x_prompt.dtype)
    y_prompt, conv_p, pool_p = trunk(x_prompt, zero_conv, zero_pool, 0, weights)
    y_sample, conv_s, pool_s = trunk(x_sample, state_conv, state_pool, PAST_LEN, weights)
    return (y_prompt, y_sample, conv_p, pool_p, conv_s, pool_s)
```

```python
import functools

import jax
import jax.numpy as jnp
from jax import lax
from jax.experimental import pallas as pl
from jax.experimental.pallas import tpu as pltpu

POOL_WINDOWS = (2, 4, 8, 16)
PAST_LEN = 2048
TOP_K = 2
EPS = 1e-6
LANES = 128
SUBLANES = 8
VMEM_LIMIT_BYTES = 56 * 1024 * 1024

_F32 = jnp.float32
_BF16 = jnp.bfloat16


def _round_up(n, m):
    return (n + m - 1) // m * m


def _rms_norm(x, g):
    ms = jnp.mean(x * x, axis=-1, keepdims=True)
    return x * lax.rsqrt(ms + EPS) * g


def _const_spec(shape):
    nd = len(shape)
    return pl.BlockSpec(shape, lambda *_: (0,) * nd, pipeline_mode=pl.Buffered(1))


def _mixer_kernel(x_ref, ch_ref, ph_ref, ng_ref, win_ref, cw_ref, cb_ref, lg_ref, lb_ref,
                  pw_ref, ps_ref, wout_ref,
                  y_ref, nc_ref, np_ref,
                  exta_ref, extp_ref, mix_ref, d_ref,
                  *, pos0, tl, rb, n_t):
    t = pl.program_id(1)
    n_cc = exta_ref.shape[0]
    n_g = extp_ref.shape[0]
    cc = n_cc * LANES
    k_taps = cw_ref.shape[0]
    kh = k_taps - 1
    ha = exta_ref.shape[1] - tl
    ph = ph_ref.shape[1]
    hp = extp_ref.shape[1] - tl

    @pl.when(t == 0)
    def _load_state():
        for j in range(n_cc):
            exta_ref[j, ha - kh:ha, :] = ch_ref[0, :, j * LANES:(j + 1) * LANES]
        for g in range(n_g):
            extp_ref[g, hp - ph:hp, :] = ph_ref[0, :, g * LANES:(g + 1) * LANES]

    h = _rms_norm(x_ref[0], ng_ref[...]).astype(_BF16)
    u = jnp.dot(h, win_ref[...], preferred_element_type=_F32)
    a = u[:, :cc] * jax.nn.sigmoid(u[:, cc:2 * cc])
    for j in range(n_cc):
        exta_ref[j, ha:ha + tl, :] = a[:, j * LANES:(j + 1) * LANES]
    for g in range(n_g):
        extp_ref[g, hp:hp + tl, :] = u[:, 2 * cc + g * LANES:2 * cc + (g + 1) * LANES]

    def row_block(i, carry):
        r0 = pl.multiple_of(i * rb, rb)
        ys = []
        for j in range(n_cc):
            ls = slice(j * LANES, (j + 1) * LANES)
            acc = jnp.zeros((rb, LANES), _F32)
            for k in range(k_taps):
                acc = acc + cw_ref[k:k + 1, ls] * exta_ref[j, pl.ds(r0 + (ha - kh + k), rb), :]
            ys.append(acc + cb_ref[:, ls])
        tot = ys[0]
        for y in ys[1:]:
            tot = tot + y
        mu = jnp.sum(tot, axis=-1, keepdims=True) * (1.0 / cc)
        xc = [y - mu for y in ys]
        sq = xc[0] * xc[0]
        for c in xc[1:]:
            sq = sq + c * c
        inv = lax.rsqrt(jnp.sum(sq, axis=-1, keepdims=True) * (1.0 / cc) + EPS)
        for j in range(n_cc):
            ls = slice(j * LANES, (j + 1) * LANES)
            v = xc[j] * inv * lg_ref[:, ls] + lb_ref[:, ls]
            mix_ref[pl.ds(r0, rb), ls] = (v * jax.nn.sigmoid(v)).astype(_BF16)
        row = lax.broadcasted_iota(jnp.int32, (rb, 1), 0)
        pos1 = (row + (pos0 + 1 + t * tl + r0)).astype(_F32)
        for g, w in enumerate(POOL_WINDOWS):
            cur = extp_ref[g, pl.ds(r0 + hp, rb), :]
            s = cur
            for back in range(1, w):
                s = s + extp_ref[g, pl.ds(r0 + (hp - back), rb), :]
            d = s / jnp.minimum(pos1, float(w)) - cur
            d_ref[pl.ds(r0, rb), g * LANES:(g + 1) * LANES] = d.astype(_BF16)
        return carry

    lax.fori_loop(0, tl // rb, row_block, 0)

    for g in range(n_g):
        ls = slice(g * LANES, (g + 1) * LANES)
        z = jnp.dot(d_ref[:, ls], pw_ref[g], preferred_element_type=_F32) * ps_ref[:, ls]
        mix_ref[:, cc + g * LANES:cc + (g + 1) * LANES] = z.astype(_BF16)
    y_ref[0] = x_ref[0] + jnp.dot(mix_ref[...], wout_ref[...], preferred_element_type=_F32)

    for j in range(n_cc):
        nc_ref[0, :, j * LANES:(j + 1) * LANES] = exta_ref[j, ha + tl - kh:ha + tl, :]
    for g in range(n_g):
        np_ref[0, :, g * LANES:(g + 1) * LANES] = extp_ref[g, hp + tl - ph:hp + tl, :]
    if n_t > 1:
        for j in range(n_cc):
            exta_ref[j, ha - kh:ha, :] = exta_ref[j, ha + tl - kh:ha + tl, :]
        for g in range(n_g):
            extp_ref[g, hp - ph:hp, :] = extp_ref[g, hp + tl - ph:hp + tl, :]


def _mixer_layer(x, conv_hist, pool_hist, pos0, w, *, tl):
    b, l, d = x.shape
    kh, cc = conv_hist.shape[1], conv_hist.shape[2]
    ph, pc = pool_hist.shape[1], pool_hist.shape[2]
    n_g = len(POOL_WINDOWS)
    assert cc % LANES == 0 and pc == n_g * LANES, (cc, pc)
    assert ph >= max(POOL_WINDOWS) - 1 and w['conv_w'].shape[0] == kh + 1
    tl = min(tl, l)
    n_t = l // tl
    assert l % tl == 0 and (n_t == 1 or tl >= max(kh, ph))
    rb = min(32, tl)
    assert tl % rb == 0 and rb % 16 == 0
    ha, hp = _round_up(kh, SUBLANES), _round_up(ph, SUBLANES)

    kern = functools.partial(_mixer_kernel, pos0=pos0, tl=tl, rb=rb, n_t=n_t)
    seq_spec = lambda rows, ch: pl.BlockSpec((1, rows, ch), lambda i, j: (i, 0, 0))
    return pl.pallas_call(
        kern,
        grid=(b, n_t),
        in_specs=[
            pl.BlockSpec((1, tl, d), lambda i, j: (i, j, 0)),
            seq_spec(kh, cc), seq_spec(ph, pc),
            _const_spec((1, d)), _const_spec(w['w_in'].shape),
            _const_spec(w['conv_w'].shape), _const_spec((1, cc)), _const_spec((1, cc)), _const_spec((1, cc)),
            _const_spec(w['pool_w'].shape), _const_spec((1, pc)), _const_spec(w['w_out'].shape),
        ],
        out_specs=[
            pl.BlockSpec((1, tl, d), lambda i, j: (i, j, 0)),
            seq_spec(kh, cc), seq_spec(ph, pc),
        ],
        out_shape=[
            jax.ShapeDtypeStruct((b, l, d), _F32),
            jax.ShapeDtypeStruct((b, kh, cc), _F32),
            jax.ShapeDtypeStruct((b, ph, pc), _F32),
        ],
        scratch_shapes=[
            pltpu.VMEM((cc // LANES, ha + tl, LANES), _F32),
            pltpu.VMEM((n_g, hp + tl, LANES), _F32),
            pltpu.VMEM((tl, cc + pc), _BF16),
            pltpu.VMEM((tl, pc), _BF16),
        ],
        compiler_params=pltpu.CompilerParams(
            dimension_semantics=("arbitrary", "arbitrary"), vmem_limit_bytes=VMEM_LIMIT_BYTES),
        name="mixer",
    )(x, conv_hist, pool_hist, w['norm_mix'], w['w_in'], w['conv_w'], w['conv_b'], w['conv_ln_g'],
      w['conv_ln_b'], w['pool_w'], w['pool_scale'], w['w_out'])


def _swiglu(h, wg, wu, wd):
    g = jnp.dot(h, wg, preferred_element_type=_F32)
    u = jnp.dot(h, wu, preferred_element_type=_F32)
    act = (g * jax.nn.sigmoid(g) * u).astype(_BF16)
    return jnp.dot(act, wd, preferred_element_type=_F32)


def _dense_ffn_kernel(x_ref, ng_ref, wg_ref, wu_ref, wd_ref, o_ref):
    x = x_ref[...]
    h = _rms_norm(x, ng_ref[...]).astype(_BF16)
    o_ref[...] = x + _swiglu(h, wg_ref[...], wu_ref[...], wd_ref[...])


def _dense_ffn(x, norm_g, wg, wu, wd, *, tm):
    t, d = x.shape
    tm = min(tm, t)
    assert t % tm == 0
    return pl.pallas_call(
        _dense_ffn_kernel,
        grid=(t // tm,),
        in_specs=[pl.BlockSpec((tm, d), lambda i: (i, 0)), _const_spec((1, d)),
                  _const_spec(wg.shape), _const_spec(wu.shape), _const_spec(wd.shape)],
        out_specs=pl.BlockSpec((tm, d), lambda i: (i, 0)),
        out_shape=jax.ShapeDtypeStruct((t, d), _F32),
        compiler_params=pltpu.CompilerParams(
            dimension_semantics=("arbitrary",), vmem_limit_bytes=VMEM_LIMIT_BYTES),
        name="dense_ffn",
    )(x, norm_g, wg, wu, wd)


def _router_kernel(x_ref, ng_ref, rw_ref, rb_ref, gates_ref, *, n_experts):
    h = _rms_norm(x_ref[...], ng_ref[...])
    logits = jnp.dot(h, rw_ref[...], preferred_element_type=_F32,
                     precision=lax.Precision.HIGHEST) + rb_ref[...]
    lane = lax.broadcasted_iota(jnp.int32, logits.shape, 1)
    neg = jnp.float32(-jnp.inf)
    logits = jnp.where(lane < n_experts, logits, neg)
    m1 = jnp.max(logits, axis=-1, keepdims=True)
    i1 = jnp.min(jnp.where(logits == m1, lane, LANES), axis=-1, keepdims=True)
    rest = jnp.where(lane == i1, neg, logits)
    m2 = jnp.max(rest, axis=-1, keepdims=True)
    i2 = jnp.min(jnp.where(rest == m2, lane, LANES), axis=-1, keepdims=True)
    e2 = jnp.exp(m2 - m1)
    den = 1.0 + e2
    gates_ref[...] = jnp.where(lane == i1, 1.0 / den, 0.0) + jnp.where(lane == i2, e2 / den, 0.0)


def _router(x, norm_g, router_w, router_b, *, tm):
    t, d = x.shape
    n_experts = router_w.shape[1]
    tm = min(tm, t)
    assert t % tm == 0 and TOP_K <= n_experts <= LANES
    rw = jnp.zeros((d, LANES), _F32).at[:, :n_experts].set(router_w)
    rb = jnp.zeros((1, LANES), _F32).at[0, :n_experts].set(router_b)
    return pl.pallas_call(
        functools.partial(_router_kernel, n_experts=n_experts),
        grid=(t // tm,),
        in_specs=[pl.BlockSpec((tm, d), lambda i: (i, 0)), _const_spec((1, d)),
                  _const_spec((d, LANES)), _const_spec((1, LANES))],
        out_specs=pl.BlockSpec((tm, LANES), lambda i: (i, 0)),
        out_shape=jax.ShapeDtypeStruct((t, LANES), _F32),
        compiler_params=pltpu.CompilerParams(
            dimension_semantics=("arbitrary",), vmem_limit_bytes=VMEM_LIMIT_BYTES),
        name="router",
    )(x, norm_g, rw, rb)


def _moe_ffn_kernel(x_ref, ng_ref, gates_ref, wg_ref, wu_ref, wd_ref, o_ref, h_ref):
    e = pl.program_id(1)

    @pl.when(e == 0)
    def _init():
        x = x_ref[...]
        h_ref[...] = _rms_norm(x, ng_ref[...]).astype(_BF16)
        o_ref[...] = x

    lane = lax.broadcasted_iota(jnp.int32, gates_ref.shape, 1)
    gate = jnp.sum(jnp.where(lane == e, gates_ref[...], 0.0), axis=-1, keepdims=True)
    o_ref[...] += gate * _swiglu(h_ref[...], wg_ref[0], wu_ref[0], wd_ref[0])


def _moe_ffn(x, norm_g, gates, wg, wu, wd, *, tm):
    t, d = x.shape
    n_experts, _, ff = wg.shape
    tm = min(tm, t)
    assert t % tm == 0
    return pl.pallas_call(
        _moe_ffn_kernel,
        grid=(t // tm, n_experts),
        in_specs=[pl.BlockSpec((tm, d), lambda i, e: (i, 0)), _const_spec((1, d)),
                  pl.BlockSpec((tm, LANES), lambda i, e: (i, 0)),
                  pl.BlockSpec((1, d, ff), lambda i, e: (e, 0, 0)),
                  pl.BlockSpec((1, d, ff), lambda i, e: (e, 0, 0)),
                  pl.BlockSpec((1, ff, d), lambda i, e: (e, 0, 0))],
        out_specs=pl.BlockSpec((tm, d), lambda i, e: (i, 0)),
        out_shape=jax.ShapeDtypeStruct((t, d), _F32),
        scratch_shapes=[pltpu.VMEM((tm, d), _BF16)],
        compiler_params=pltpu.CompilerParams(
            dimension_semantics=("arbitrary", "arbitrary"), vmem_limit_bytes=VMEM_LIMIT_BYTES),
        name="moe_ffn",
    )(x, norm_g, gates, wg, wu, wd)


def _final_norm_kernel(x_ref, g_ref, o_ref):
    o_ref[...] = _rms_norm(x_ref[...], g_ref[...])


def _final_norm(x, g, *, tm):
    t, d = x.shape
    tm = min(tm, t)
    assert t % tm == 0
    return pl.pallas_call(
        _final_norm_kernel,
        grid=(t // tm,),
        in_specs=[pl.BlockSpec((tm, d), lambda i: (i, 0)), _const_spec((1, d))],
        out_specs=pl.BlockSpec((tm, d), lambda i: (i, 0)),
        out_shape=jax.ShapeDtypeStruct((t, d), _F32),
        compiler_params=pltpu.CompilerParams(dimension_semantics=("arbitrary",)),
        name="final_norm",
    )(x, g)


def _trunk(x, conv_hist, pool_hist, pos0, layers, norm_final, *, tl, tm):
    b, l, d = x.shape
    new_conv, new_pool = [], []
    for li, lw in enumerate(layers):
        x, nc, npl = _mixer_layer(x, conv_hist[li], pool_hist[li], pos0, lw, tl=tl)
        new_conv.append(nc)
        new_pool.append(npl)
        xf = x.reshape(b * l, d)
        if 'router_w' in lw:
            gates = _router(xf, lw['norm_ffn'], lw['router_w'], lw['router_b'], tm=tm)
            xf = _moe_ffn(xf, lw['norm_ffn'], gates, lw['wg'], lw['wu'], lw['wd'], tm=tm // 2)
        else:
            xf = _dense_ffn(xf, lw['norm_ffn'], lw['wg'], lw['wu'], lw['wd'], tm=tm)
        x = xf.reshape(b, l, d)
    y = _final_norm(x.reshape(b * l, d), norm_final, tm=tm).reshape(b, l, d)
    return y, jnp.stack(new_conv), jnp.stack(new_pool)


def kernel(x_prompt, x_sample, state_conv, state_pool, norm_mix, w_in, conv_w, conv_b, conv_ln_g, conv_ln_b, pool_w, pool_scale, w_out, norm_ffn, dense_w_gate, dense_w_up, dense_w_down, router_w, router_b, moe_w_gate, moe_w_up, moe_w_down, norm_final):
    depth = w_in.shape[0]
    row = lambda v: v.reshape(1, -1)
    layers = []
    for li in range(depth):
        lw = dict(
            norm_mix=row(norm_mix[li]), w_in=w_in[li].astype(_BF16), conv_w=conv_w[li],
            conv_b=row(conv_b[li]), conv_ln_g=row(conv_ln_g[li]), conv_ln_b=row(conv_ln_b[li]),
            pool_w=pool_w[li].astype(_BF16), pool_scale=row(pool_scale[li]),
            w_out=w_out[li].astype(_BF16), norm_ffn=row(norm_ffn[li]))
        j = li // 2
        if li % 2 == 0:
            lw.update(wg=dense_w_gate[j].astype(_BF16), wu=dense_w_up[j].astype(_BF16),
                      wd=dense_w_down[j].astype(_BF16))
        else:
            lw.update(router_w=router_w[j], router_b=router_b[j], wg=moe_w_gate[j].astype(_BF16),
                      wu=moe_w_up[j].astype(_BF16), wd=moe_w_down[j].astype(_BF16))
        layers.append(lw)
    nf = row(norm_final)

    bp = x_prompt.shape[0]
    zero_conv = jnp.zeros((depth, bp) + state_conv.shape[2:], x_prompt.dtype)
    zero_pool = jnp.zeros((depth, bp) + state_pool.shape[2:], x_prompt.dtype)
    y_p, conv_p, pool_p = _trunk(x_prompt, zero_conv, zero_pool, 0, layers, nf, tl=512, tm=512)
    y_s, conv_s, pool_s = _trunk(x_sample, state_conv, state_pool, PAST_LEN, layers, nf, tl=512, tm=512)
    return (y_p, y_s, conv_p, pool_p, conv_s, pool_s)
```

```python
import functools

import jax
import jax.numpy as jnp
from jax import lax
from jax.experimental import pallas as pl
from jax.experimental.pallas import tpu as pltpu

POOL_WINDOWS = (2, 4, 8, 16)
PAST_LEN = 2048
TOP_K = 2
EPS = 1e-6
LANES = 128
SUBLANES = 8
VMEM_LIMIT_BYTES = 56 * 1024 * 1024
TILES = dict(tl=512, tm=512, tg=512, tr=1024)

_F32 = jnp.float32
_BF16 = jnp.bfloat16


def _round_up(n, m):
    return (n + m - 1) // m * m


def _rms_norm(x, g):
    ms = jnp.mean(x * x, axis=-1, keepdims=True)
    return x * lax.rsqrt(ms + EPS) * g


def _const_spec(shape):
    nd = len(shape)
    return pl.BlockSpec(shape, lambda *_: (0,) * nd, pipeline_mode=pl.Buffered(1))


def _mixer_kernel(x_ref, ch_ref, ph_ref, ng_ref, win_ref, cw_ref, cb_ref, lg_ref, lb_ref,
                  pw_ref, ps_ref, wout_ref,
                  y_ref, nc_ref, np_ref,
                  exta_ref, extp_ref, mix_ref, d_ref,
                  *, pos0, tl, rb, n_t):
    t = pl.program_id(1)
    n_cc = exta_ref.shape[0]
    n_g = extp_ref.shape[0]
    cc = n_cc * LANES
    k_taps = cw_ref.shape[0]
    kh = k_taps - 1
    ha = exta_ref.shape[1] - tl
    ph = ph_ref.shape[1]
    hp = extp_ref.shape[1] - tl

    @pl.when(t == 0)
    def _load_state():
        for j in range(n_cc):
            exta_ref[j, ha - kh:ha, :] = ch_ref[0, :, j * LANES:(j + 1) * LANES]
        for g in range(n_g):
            extp_ref[g, hp - ph:hp, :] = ph_ref[0, :, g * LANES:(g + 1) * LANES]

    h = _rms_norm(x_ref[0], ng_ref[...]).astype(_BF16)
    u = jnp.dot(h, win_ref[...], preferred_element_type=_F32)
    a = u[:, :cc] * jax.nn.sigmoid(u[:, cc:2 * cc])
    for j in range(n_cc):
        exta_ref[j, ha:ha + tl, :] = a[:, j * LANES:(j + 1) * LANES]
    for g in range(n_g):
        extp_ref[g, hp:hp + tl, :] = u[:, 2 * cc + g * LANES:2 * cc + (g + 1) * LANES]

    def row_block(i, carry):
        r0 = pl.multiple_of(i * rb, rb)
        ys = []
        for j in range(n_cc):
            ls = slice(j * LANES, (j + 1) * LANES)
            acc = jnp.zeros((rb, LANES), _F32)
            for k in range(k_taps):
                acc = acc + cw_ref[k:k + 1, ls] * exta_ref[j, pl.ds(r0 + (ha - kh + k), rb), :]
            ys.append(acc + cb_ref[:, ls])
        tot = ys[0]
        for y in ys[1:]:
            tot = tot + y
        mu = jnp.sum(tot, axis=-1, keepdims=True) * (1.0 / cc)
        xc = [y - mu for y in ys]
        sq = xc[0] * xc[0]
        for c in xc[1:]:
            sq = sq + c * c
        inv = lax.rsqrt(jnp.sum(sq, axis=-1, keepdims=True) * (1.0 / cc) + EPS)
        for j in range(n_cc):
            ls = slice(j * LANES, (j + 1) * LANES)
            v = xc[j] * inv * lg_ref[:, ls] + lb_ref[:, ls]
            mix_ref[pl.ds(r0, rb), ls] = (v * jax.nn.sigmoid(v)).astype(_BF16)
        row = lax.broadcasted_iota(jnp.int32, (rb, 1), 0)
        pos1 = (row + (pos0 + 1 + t * tl + r0)).astype(_F32)
        for g, w in enumerate(POOL_WINDOWS):
            cur = extp_ref[g, pl.ds(r0 + hp, rb), :]
            s = cur
            for back in range(1, w):
                s = s + extp_ref[g, pl.ds(r0 + (hp - back), rb), :]
            d = s / jnp.minimum(pos1, float(w)) - cur
            d_ref[pl.ds(r0, rb), g * LANES:(g + 1) * LANES] = d.astype(_BF16)
        return carry

    lax.fori_loop(0, tl // rb, row_block, 0)

    for g in range(n_g):
        ls = slice(g * LANES, (g + 1) * LANES)
        z = jnp.dot(d_ref[:, ls], pw_ref[g], preferred_element_type=_F32) * ps_ref[:, ls]
        mix_ref[:, cc + g * LANES:cc + (g + 1) * LANES] = z.astype(_BF16)
    y_ref[0] = x_ref[0] + jnp.dot(mix_ref[...], wout_ref[...], preferred_element_type=_F32)

    for j in range(n_cc):
        nc_ref[0, :, j * LANES:(j + 1) * LANES] = exta_ref[j, ha + tl - kh:ha + tl, :]
    for g in range(n_g):
        np_ref[0, :, g * LANES:(g + 1) * LANES] = extp_ref[g, hp + tl - ph:hp + tl, :]
    if n_t > 1:
        for j in range(n_cc):
            exta_ref[j, ha - kh:ha, :] = exta_ref[j, ha + tl - kh:ha + tl, :]
        for g in range(n_g):
            extp_ref[g, hp - ph:hp, :] = extp_ref[g, hp + tl - ph:hp + tl, :]


def _mixer_layer(x, conv_hist, pool_hist, pos0, w, *, tl):
    b, l, d = x.shape
    kh, cc = conv_hist.shape[1], conv_hist.shape[2]
    ph, pc = pool_hist.shape[1], pool_hist.shape[2]
    n_g = len(POOL_WINDOWS)
    assert cc % LANES == 0 and pc == n_g * LANES, (cc, pc)
    assert ph >= max(POOL_WINDOWS) - 1 and w['conv_w'].shape[0] == kh + 1
    tl = min(tl, l)
    n_t = l // tl
    assert l % tl == 0 and (n_t == 1 or tl >= max(kh, ph))
    rb = min(32, tl)
    assert tl % rb == 0 and rb % 16 == 0
    ha, hp = _round_up(kh, SUBLANES), _round_up(ph, SUBLANES)

    kern = functools.partial(_mixer_kernel, pos0=pos0, tl=tl, rb=rb, n_t=n_t)
    seq_spec = lambda rows, ch: pl.BlockSpec((1, rows, ch), lambda i, j: (i, 0, 0))
    return pl.pallas_call(
        kern,
        grid=(b, n_t),
        in_specs=[
            pl.BlockSpec((1, tl, d), lambda i, j: (i, j, 0)),
            seq_spec(kh, cc), seq_spec(ph, pc),
            _const_spec((1, d)), _const_spec(w['w_in'].shape),
            _const_spec(w['conv_w'].shape), _const_spec((1, cc)), _const_spec((1, cc)), _const_spec((1, cc)),
            _const_spec(w['pool_w'].shape), _const_spec((1, pc)), _const_spec(w['w_out'].shape),
        ],
        out_specs=[
            pl.BlockSpec((1, tl, d), lambda i, j: (i, j, 0)),
            seq_spec(kh, cc), seq_spec(ph, pc),
        ],
        out_shape=[
            jax.ShapeDtypeStruct((b, l, d), _F32),
            jax.ShapeDtypeStruct((b, kh, cc), _F32),
            jax.ShapeDtypeStruct((b, ph, pc), _F32),
        ],
        scratch_shapes=[
            pltpu.VMEM((cc // LANES, ha + tl, LANES), _F32),
            pltpu.VMEM((n_g, hp + tl, LANES), _F32),
            pltpu.VMEM((tl, cc + pc), _BF16),
            pltpu.VMEM((tl, pc), _BF16),
        ],
        compiler_params=pltpu.CompilerParams(
            dimension_semantics=("arbitrary", "arbitrary"), vmem_limit_bytes=VMEM_LIMIT_BYTES),
        name="mixer",
    )(x, conv_hist, pool_hist, w['norm_mix'], w['w_in'], w['conv_w'], w['conv_b'], w['conv_ln_g'],
      w['conv_ln_b'], w['pool_w'], w['pool_scale'], w['w_out'])


def _swiglu(h, wg, wu, wd):
    g = jnp.dot(h, wg, preferred_element_type=_F32)
    u = jnp.dot(h, wu, preferred_element_type=_F32)
    act = (g * jax.nn.sigmoid(g) * u).astype(_BF16)
    return jnp.dot(act, wd, preferred_element_type=_F32)


def _dense_ffn_kernel(x_ref, ng_ref, wg_ref, wu_ref, wd_ref, o_ref):
    x = x_ref[...]
    h = _rms_norm(x, ng_ref[...]).astype(_BF16)
    o_ref[...] = x + _swiglu(h, wg_ref[...], wu_ref[...], wd_ref[...])


def _dense_ffn(x, norm_g, wg, wu, wd, *, tm):
    t, d = x.shape
    assert t % tm == 0
    return pl.pallas_call(
        _dense_ffn_kernel,
        grid=(t // tm,),
        in_specs=[pl.BlockSpec((tm, d), lambda i: (i, 0)), _const_spec((1, d)),
                  _const_spec(wg.shape), _const_spec(wu.shape), _const_spec(wd.shape)],
        out_specs=pl.BlockSpec((tm, d), lambda i: (i, 0)),
        out_shape=jax.ShapeDtypeStruct((t, d), _F32),
        compiler_params=pltpu.CompilerParams(
            dimension_semantics=("arbitrary",), vmem_limit_bytes=VMEM_LIMIT_BYTES),
        name="dense_ffn",
    )(x, norm_g, wg, wu, wd)


def _router_kernel(x_ref, ng_ref, rw_ref, rb_ref, route_ref, *, n_experts):
    h = _rms_norm(x_ref[...], ng_ref[...]).astype(_BF16)
    logits = jnp.dot(h, rw_ref[...], preferred_element_type=_F32) + rb_ref[...]
    lane = lax.broadcasted_iota(jnp.int32, logits.shape, 1)
    neg = jnp.float32(-jnp.inf)
    logits = jnp.where(lane < n_experts, logits, neg)
    m1 = jnp.max(logits, axis=-1, keepdims=True)
    i1 = jnp.min(jnp.where(logits == m1, lane, LANES), axis=-1, keepdims=True)
    rest = jnp.where(lane == i1, neg, logits)
    m2 = jnp.max(rest, axis=-1, keepdims=True)
    i2 = jnp.min(jnp.where(rest == m2, lane, LANES), axis=-1, keepdims=True)
    e2 = jnp.exp(m2 - m1)
    den = 1.0 + e2
    route = jnp.where(lane == 0, i1.astype(_F32), 0.0)
    route = jnp.where(lane == 1, i2.astype(_F32), route)
    route = jnp.where(lane == 2, 1.0 / den, route)
    route_ref[...] = jnp.where(lane == 3, e2 / den, route)


def _router(x, norm_g, router_w, router_b, *, tm):
    t, d = x.shape
    n_experts = router_w.shape[1]
    assert t % tm == 0 and TOP_K <= n_experts <= LANES
    rw = jnp.zeros((d, LANES), _BF16).at[:, :n_experts].set(router_w.astype(_BF16))
    rb = jnp.zeros((1, LANES), _F32).at[0, :n_experts].set(router_b)
    return pl.pallas_call(
        functools.partial(_router_kernel, n_experts=n_experts),
        grid=(t // tm,),
        in_specs=[pl.BlockSpec((tm, d), lambda i: (i, 0)), _const_spec((1, d)),
                  _const_spec((d, LANES)), _const_spec((1, LANES))],
        out_specs=pl.BlockSpec((tm, LANES), lambda i: (i, 0)),
        out_shape=jax.ShapeDtypeStruct((t, LANES), _F32),
        compiler_params=pltpu.CompilerParams(
            dimension_semantics=("arbitrary",), vmem_limit_bytes=VMEM_LIMIT_BYTES),
        name="router",
    )(x, norm_g, rw, rb)


def _route_plan(i1, i2, n_experts, tg):
    t = i1.shape[0]
    e_pair = jnp.concatenate([i1, i2])
    ids = jnp.arange(n_experts, dtype=jnp.int32)
    onehot = (e_pair[:, None] == ids[None, :]).astype(jnp.int32)
    csum = jnp.cumsum(onehot, axis=0)
    rank = jnp.sum(csum * onehot, axis=1) - 1
    tiles_e = (csum[-1] + (tg - 1)) // tg
    tile_end = jnp.cumsum(tiles_e)
    start_row = (tile_end - tiles_e) * tg
    pos = jnp.sum(onehot * start_row[None, :], axis=1) + rank
    n_tiles = -(-2 * t // tg) + n_experts
    n_used = tile_end[-1]
    tile_ids = jnp.arange(n_tiles, dtype=jnp.int32)
    tile_e = jnp.sum((jnp.minimum(tile_ids, n_used - 1)[:, None] >= tile_end[None, :]).astype(jnp.int32), axis=1)
    return pos.reshape(2, t), tile_e.astype(jnp.int32), n_used.reshape(1).astype(jnp.int32)


def _row_copies(idx_ref, src_of, dst_of, sem, n, wait):
    def body(r, carry):
        for k in range(TOP_K):
            row = idx_ref[k * n + r]
            cp = pltpu.make_async_copy(src_of(k, r, row), dst_of(k, r, row), sem)
            if wait:
                cp.wait()
            else:
                cp.start()
        return carry
    lax.fori_loop(0, n, body, 0, unroll=8)


def _dispatch_kernel(pos_hbm, x_hbm, xs_in_hbm, xs_hbm, idx_ref, sem_idx, sem_rows, *, tm):
    del xs_in_hbm
    i = pl.program_id(0)
    cp = pltpu.make_async_copy(pos_hbm.at[i], idx_ref, sem_idx)
    cp.start()
    cp.wait()
    base = i * tm
    src_of = lambda k, r, row: x_hbm.at[pl.ds(base + r, 1)]
    dst_of = lambda k, r, row: xs_hbm.at[pl.ds(row, 1)]
    _row_copies(idx_ref, src_of, dst_of, sem_rows, tm, wait=False)
    _row_copies(idx_ref, src_of, dst_of, sem_rows, tm, wait=True)


def _dispatch(x, pos_tiles, n_rows, *, tm):
    t, d = x.shape
    return pl.pallas_call(
        functools.partial(_dispatch_kernel, tm=tm),
        grid=(t // tm,),
        in_specs=[pl.BlockSpec(memory_space=pl.ANY)] * 3,
        out_specs=pl.BlockSpec(memory_space=pl.ANY),
        out_shape=jax.ShapeDtypeStruct((n_rows, d), _F32),
        scratch_shapes=[pltpu.SMEM((TOP_K * tm,), jnp.int32), pltpu.SemaphoreType.DMA,
                        pltpu.SemaphoreType.DMA],
        input_output_aliases={2: 0},
        compiler_params=pltpu.CompilerParams(dimension_semantics=("arbitrary",)),
        name="dispatch",
    )(pos_tiles, x, jnp.zeros((n_rows, d), _F32))


def _grouped_ffn_kernel(tile_e_ref, n_used_ref, xs_ref, ng_ref, wg_ref, wu_ref, wd_ref, ys_ref):
    del tile_e_ref

    @pl.when(pl.program_id(0) < n_used_ref[0])
    def _compute():
        h = _rms_norm(xs_ref[...], ng_ref[...]).astype(_BF16)
        ys_ref[...] = _swiglu(h, wg_ref[0], wu_ref[0], wd_ref[0])


def _grouped_ffn(xs, norm_g, tile_e, n_used, wg, wu, wd, *, tg):
    n_rows, d = xs.shape
    _, _, ff = wg.shape
    row_map = lambda i, te, nu: (jnp.minimum(i, nu[0] - 1), 0)
    w_map = lambda i, te, nu: (te[i], 0, 0)
    w_spec = lambda shape: pl.BlockSpec(shape, w_map, pipeline_mode=pl.Buffered(1))
    return pl.pallas_call(
        _grouped_ffn_kernel,
        grid_spec=pltpu.PrefetchScalarGridSpec(
            num_scalar_prefetch=2,
            grid=(n_rows // tg,),
            in_specs=[pl.BlockSpec((tg, d), row_map),
                      pl.BlockSpec((1, d), lambda i, te, nu: (0, 0), pipeline_mode=pl.Buffered(1)),
                      w_spec((1, d, ff)), w_spec((1, d, ff)), w_spec((1, ff, d))],
            out_specs=pl.BlockSpec((tg, d), row_map)),
        out_shape=jax.ShapeDtypeStruct((n_rows, d), _F32),
        compiler_params=pltpu.CompilerParams(
            dimension_semantics=("arbitrary",), vmem_limit_bytes=VMEM_LIMIT_BYTES),
        name="grouped_ffn",
    )(tile_e, n_used, xs, norm_g, wg, wu, wd)


def _combine_kernel(pos_hbm, x_ref, route_ref, ys_hbm, g_ref, o_ref, idx_ref, buf_ref, sem_idx, sem_rows,
                    *, tm, final_norm):
    i = pl.program_id(0)
    cp = pltpu.make_async_copy(pos_hbm.at[i], idx_ref, sem_idx)
    cp.start()
    cp.wait()
    src_of = lambda k, r, row: ys_hbm.at[pl.ds(row, 1)]
    dst_of = lambda k, r, row: buf_ref.at[k, pl.ds(r, 1)]
    _row_copies(idx_ref, src_of, dst_of, sem_rows, tm, wait=False)
    _row_copies(idx_ref, src_of, dst_of, sem_rows, tm, wait=True)
    route = route_ref[...]
    out = x_ref[...] + route[:, 2:3] * buf_ref[0] + route[:, 3:4] * buf_ref[1]
    if final_norm:
        out = _rms_norm(out, g_ref[...])
    o_ref[...] = out


def _combine(x, route, ys, pos_tiles, norm_g, *, tm, final_norm):
    t, d = x.shape
    return pl.pallas_call(
        functools.partial(_combine_kernel, tm=tm, final_norm=final_norm),
        grid=(t // tm,),
        in_specs=[pl.BlockSpec(memory_space=pl.ANY),
                  pl.BlockSpec((tm, d), lambda i: (i, 0)),
                  pl.BlockSpec((tm, LANES), lambda i: (i, 0)),
                  pl.BlockSpec(memory_space=pl.ANY),
                  _const_spec((1, d))],
        out_specs=pl.BlockSpec((tm, d), lambda i: (i, 0)),
        out_shape=jax.ShapeDtypeStruct((t, d), _F32),
        scratch_shapes=[pltpu.SMEM((TOP_K * tm,), jnp.int32), pltpu.VMEM((TOP_K, tm, d), _F32),
                        pltpu.SemaphoreType.DMA, pltpu.SemaphoreType.DMA],
        compiler_params=pltpu.CompilerParams(
            dimension_semantics=("arbitrary",), vmem_limit_bytes=VMEM_LIMIT_BYTES),
        name="combine",
    )(pos_tiles, x, route, ys, norm_g)


def _moe_ffn(x, lw, norm_final, *, tm, tg, tr):
    t, d = x.shape
    n_experts = lw['router_w'].shape[1]
    route = _router(x, lw['norm_ffn'], lw['router_w'], lw['router_b'], tm=tm)
    i1 = route[:, 0].astype(jnp.int32)
    i2 = route[:, 1].astype(jnp.int32)
    pos, tile_e, n_used = _route_plan(i1, i2, n_experts, tg)
    pos_tiles = pos.reshape(TOP_K, t // tr, tr).transpose(1, 0, 2).reshape(t // tr, TOP_K * tr)
    n_rows = tile_e.shape[0] * tg
    xs = _dispatch(x, pos_tiles, n_rows, tm=tr)
    ys = _grouped_ffn(xs, lw['norm_ffn'], tile_e, n_used, lw['wg'], lw['wu'], lw['wd'], tg=tg)
    g = lw['norm_ffn'] if norm_final is None else norm_final
    return _combine(x, route, ys, pos_tiles, g, tm=tr, final_norm=norm_final is not None)


def _final_norm_kernel(x_ref, g_ref, o_ref):
    o_ref[...] = _rms_norm(x_ref[...], g_ref[...])


def _final_norm(x, g, *, tm):
    t, d = x.shape
    assert t % tm == 0
    return pl.pallas_call(
        _final_norm_kernel,
        grid=(t // tm,),
        in_specs=[pl.BlockSpec((tm, d), lambda i: (i, 0)), _const_spec((1, d))],
        out_specs=pl.BlockSpec((tm, d), lambda i: (i, 0)),
        out_shape=jax.ShapeDtypeStruct((t, d), _F32),
        compiler_params=pltpu.CompilerParams(dimension_semantics=("arbitrary",)),
        name="final_norm",
    )(x, g)


def _trunk(x, conv_hist, pool_hist, pos0, layers, norm_final, *, tl, tm, tg, tr):
    b, l, d = x.shape
    t = b * l
    tm, tg, tr = min(tm, t), min(tg, t), min(tr, t)
    new_conv, new_pool = [], []
    for li, lw in enumerate(layers):
        x, nc, npl = _mixer_layer(x, conv_hist[li], pool_hist[li], pos0, lw, tl=tl)
        new_conv.append(nc)
        new_pool.append(npl)
        xf = x.reshape(t, d)
        last = li == len(layers) - 1
        if 'router_w' in lw:
            xf = _moe_ffn(xf, lw, norm_final if last else None, tm=tm, tg=tg, tr=tr)
        else:
            xf = _dense_ffn(xf, lw['norm_ffn'], lw['wg'], lw['wu'], lw['wd'], tm=tm)
            if last:
                xf = _final_norm(xf, norm_final, tm=tm)
        x = xf.reshape(b, l, d)
    return x, jnp.stack(new_conv), jnp.stack(new_pool)


def kernel(x_prompt, x_sample, state_conv, state_pool, norm_mix, w_in, conv_w, conv_b, conv_ln_g, conv_ln_b, pool_w, pool_scale, w_out, norm_ffn, dense_w_gate, dense_w_up, dense_w_down, router_w, router_b, moe_w_gate, moe_w_up, moe_w_down, norm_final):
    depth = w_in.shape[0]
    row = lambda v: v.reshape(1, -1)
    layers = []
    for li in range(depth):
        lw = dict(
            norm_mix=row(norm_mix[li]), w_in=w_in[li].astype(_BF16), conv_w=conv_w[li],
            conv_b=row(conv_b[li]), conv_ln_g=row(conv_ln_g[li]), conv_ln_b=row(conv_ln_b[li]),
            pool_w=pool_w[li].astype(_BF16), pool_scale=row(pool_scale[li]),
            w_out=w_out[li].astype(_BF16), norm_ffn=row(norm_ffn[li]))
        j = li // 2
        if li % 2 == 0:
            lw.update(wg=dense_w_gate[j].astype(_BF16), wu=dense_w_up[j].astype(_BF16),
                      wd=dense_w_down[j].astype(_BF16))
        else:
            lw.update(router_w=router_w[j], router_b=router_b[j], wg=moe_w_gate[j].astype(_BF16),
                      wu=moe_w_up[j].astype(_BF16), wd=moe_w_down[j].astype(_BF16))
        layers.append(lw)
    nf = row(norm_final)

    bp = x_prompt.shape[0]
    zero_conv = jnp.zeros((depth, bp) + state_conv.shape[2:], x_prompt.dtype)
    zero_pool = jnp.zeros((depth, bp) + state_pool.shape[2:], x_prompt.dtype)
    y_p, conv_p, pool_p = _trunk(x_prompt, zero_conv, zero_pool, 0, layers, nf, **TILES)
    y_s, conv_s, pool_s = _trunk(x_sample, state_conv, state_pool, PAST_LEN, layers, nf, **TILES)
    return (y_p, y_s, conv_p, pool_p, conv_s, pool_s)
```

```python
import functools

import jax
import jax.numpy as jnp
from jax import lax
from jax.experimental import pallas as pl
from jax.experimental.pallas import tpu as pltpu

POOL_WINDOWS = (2, 4, 8, 16)
PAST_LEN = 2048
TOP_K = 2
EPS = 1e-6
LANES = 128
SUBLANES = 8
VMEM_LIMIT_BYTES = 56 * 1024 * 1024
TILES = dict(tl=512, sb=128, tm=512, tg=512, tr=1024)

_F32 = jnp.float32
_BF16 = jnp.bfloat16


def _round_up(n, m):
    return (n + m - 1) // m * m


def _rms_norm(x, g):
    ms = jnp.mean(x * x, axis=-1, keepdims=True)
    return x * lax.rsqrt(ms + EPS) * g


def _const_spec(shape):
    nd = len(shape)
    return pl.BlockSpec(shape, lambda *_: (0,) * nd, pipeline_mode=pl.Buffered(1))


def _mixer_kernel(x_ref, ch_ref, ph_ref, ng_ref, win_ref, cw_ref, cb_ref, lg_ref, lb_ref,
                  pw_ref, ps_ref, wout_ref,
                  y_ref, nc_ref, np_ref,
                  *scratch, pos0, tl, sb, rb, n_t):
    n_sb = tl // sb
    exta, extp = scratch[:n_sb], scratch[n_sb:2 * n_sb]
    h_ref, mix_ref, d_ref = scratch[2 * n_sb:]
    t = pl.program_id(1)
    n_cc = exta[0].shape[0]
    n_g = extp[0].shape[0]
    cc = n_cc * LANES
    k_taps = cw_ref.shape[0]
    kh = k_taps - 1
    ha = exta[0].shape[1] - sb
    ph = ph_ref.shape[1]
    hp = extp[0].shape[1] - sb

    def copy_history(dst_a, dst_p, src_a, src_p):
        for j in range(n_cc):
            dst_a[j, ha - kh:ha, :] = src_a[j, ha + sb - kh:ha + sb, :]
        for g in range(n_g):
            dst_p[g, hp - ph:hp, :] = src_p[g, hp + sb - ph:hp + sb, :]

    @pl.when(t == 0)
    def _load_state():
        for j in range(n_cc):
            exta[0][j, ha - kh:ha, :] = ch_ref[0, :, j * LANES:(j + 1) * LANES]
        for g in range(n_g):
            extp[0][g, hp - ph:hp, :] = ph_ref[0, :, g * LANES:(g + 1) * LANES]

    def project_in(s):
        ea, ep = exta[s], extp[s]
        if s > 0:
            copy_history(ea, ep, exta[s - 1], extp[s - 1])
        u = jnp.dot(h_ref[s * sb:(s + 1) * sb, :], win_ref[...], preferred_element_type=_F32)
        a = u[:, :cc] * jax.nn.sigmoid(u[:, cc:2 * cc])
        for j in range(n_cc):
            ea[j, ha:ha + sb, :] = a[:, j * LANES:(j + 1) * LANES]
        for g in range(n_g):
            ep[g, hp:hp + sb, :] = u[:, 2 * cc + g * LANES:2 * cc + (g + 1) * LANES]

    def mix_block(s):
        s0 = s * sb
        rows = slice(s0, s0 + sb)
        ea, ep = exta[s], extp[s]
        for r0 in range(0, sb, rb):
            ys = []
            for j in range(n_cc):
                ls = slice(j * LANES, (j + 1) * LANES)
                acc = jnp.zeros((rb, LANES), _F32)
                for k in range(k_taps):
                    e0 = r0 + ha - kh + k
                    acc = acc + cw_ref[k:k + 1, ls] * ea[j, e0:e0 + rb, :]
                ys.append(acc + cb_ref[:, ls])
            tot = ys[0]
            for y in ys[1:]:
                tot = tot + y
            mu = jnp.sum(tot, axis=-1, keepdims=True) * (1.0 / cc)
            xc = [y - mu for y in ys]
            sq = xc[0] * xc[0]
            for c in xc[1:]:
                sq = sq + c * c
            inv = lax.rsqrt(jnp.sum(sq, axis=-1, keepdims=True) * (1.0 / cc) + EPS)
            for j in range(n_cc):
                ls = slice(j * LANES, (j + 1) * LANES)
                v = xc[j] * inv * lg_ref[:, ls] + lb_ref[:, ls]
                mix_ref[s0 + r0:s0 + r0 + rb, ls] = (v * jax.nn.sigmoid(v)).astype(_BF16)
            row = lax.broadcasted_iota(jnp.int32, (rb, 1), 0)
            pos1 = (row + (pos0 + 1 + t * tl + s0 + r0)).astype(_F32)
            for g, w in enumerate(POOL_WINDOWS):
                cur = ep[g, hp + r0:hp + r0 + rb, :]
                acc = cur
                for back in range(1, w):
                    acc = acc + ep[g, hp + r0 - back:hp + r0 - back + rb, :]
                d = acc / jnp.minimum(pos1, float(w)) - cur
                d_ref[s0 + r0:s0 + r0 + rb, g * LANES:(g + 1) * LANES] = d.astype(_BF16)

        for g in range(n_g):
            ls = slice(g * LANES, (g + 1) * LANES)
            z = jnp.dot(d_ref[rows, ls], pw_ref[g], preferred_element_type=_F32) * ps_ref[:, ls]
            mix_ref[rows, cc + g * LANES:cc + (g + 1) * LANES] = z.astype(_BF16)
        y_ref[0, rows, :] = x_ref[0, rows, :] + jnp.dot(mix_ref[rows, :], wout_ref[...],
                                                        preferred_element_type=_F32)

    h_ref[...] = _rms_norm(x_ref[0], ng_ref[...]).astype(_BF16)
    project_in(0)
    for s in range(n_sb):
        if s + 1 < n_sb:
            project_in(s + 1)
        mix_block(s)

    last_a, last_p = exta[n_sb - 1], extp[n_sb - 1]
    for j in range(n_cc):
        nc_ref[0, :, j * LANES:(j + 1) * LANES] = last_a[j, ha + sb - kh:ha + sb, :]
    for g in range(n_g):
        np_ref[0, :, g * LANES:(g + 1) * LANES] = last_p[g, hp + sb - ph:hp + sb, :]
    if n_t > 1:
        copy_history(exta[0], extp[0], last_a, last_p)


def _mixer_layer(x, conv_hist, pool_hist, pos0, w, *, tl, sb):
    b, l, d = x.shape
    kh, cc = conv_hist.shape[1], conv_hist.shape[2]
    ph, pc = pool_hist.shape[1], pool_hist.shape[2]
    n_g = len(POOL_WINDOWS)
    assert cc % LANES == 0 and pc == n_g * LANES, (cc, pc)
    assert ph >= max(POOL_WINDOWS) - 1 and w['conv_w'].shape[0] == kh + 1
    tl = min(tl, l)
    n_t = l // tl
    sb = min(sb, tl)
    assert l % tl == 0 and (l == sb or sb >= max(kh, ph))
    rb = min(32, sb)
    assert tl % sb == 0 and sb % rb == 0 and rb % 16 == 0
    ha, hp = _round_up(kh, SUBLANES), _round_up(ph, SUBLANES)

    kern = functools.partial(_mixer_kernel, pos0=pos0, tl=tl, sb=sb, rb=rb, n_t=n_t)
    seq_spec = lambda rows, ch: pl.BlockSpec((1, rows, ch), lambda i, j: (i, 0, 0))
    return pl.pallas_call(
        kern,
        grid=(b, n_t),
        in_specs=[
            pl.BlockSpec((1, tl, d), lambda i, j: (i, j, 0)),
            seq_spec(kh, cc), seq_spec(ph, pc),
            _const_spec((1, d)), _const_spec(w['w_in'].shape),
            _const_spec(w['conv_w'].shape), _const_spec((1, cc)), _const_spec((1, cc)), _const_spec((1, cc)),
            _const_spec(w['pool_w'].shape), _const_spec((1, pc)), _const_spec(w['w_out'].shape),
        ],
        out_specs=[
            pl.BlockSpec((1, tl, d), lambda i, j: (i, j, 0)),
            seq_spec(kh, cc), seq_spec(ph, pc),
        ],
        out_shape=[
            jax.ShapeDtypeStruct((b, l, d), _F32),
            jax.ShapeDtypeStruct((b, kh, cc), _F32),
            jax.ShapeDtypeStruct((b, ph, pc), _F32),
        ],
        scratch_shapes=(
            [pltpu.VMEM((cc // LANES, ha + sb, LANES), _F32)] * (tl // sb)
            + [pltpu.VMEM((n_g, hp + sb, LANES), _F32)] * (tl // sb)
            + [pltpu.VMEM((tl, d), _BF16), pltpu.VMEM((tl, cc + pc), _BF16), pltpu.VMEM((tl, pc), _BF16)]),
        compiler_params=pltpu.CompilerParams(
            dimension_semantics=("arbitrary", "arbitrary"), vmem_limit_bytes=VMEM_LIMIT_BYTES),
        name="mixer",
    )(x, conv_hist, pool_hist, w['norm_mix'], w['w_in'], w['conv_w'], w['conv_b'], w['conv_ln_g'],
      w['conv_ln_b'], w['pool_w'], w['pool_scale'], w['w_out'])


def _swiglu(h, wg, wu, wd):
    g = jnp.dot(h, wg, preferred_element_type=_F32)
    u = jnp.dot(h, wu, preferred_element_type=_F32)
    act = (g * jax.nn.sigmoid(g) * u).astype(_BF16)
    return jnp.dot(act, wd, preferred_element_type=_F32)


def _dense_ffn_kernel(x_ref, ng_ref, wg_ref, wu_ref, wd_ref, o_ref):
    x = x_ref[...]
    h = _rms_norm(x, ng_ref[...]).astype(_BF16)
    o_ref[...] = x + _swiglu(h, wg_ref[...], wu_ref[...], wd_ref[...])


def _dense_ffn(x, norm_g, wg, wu, wd, *, tm):
    t, d = x.shape
    assert t % tm == 0
    return pl.pallas_call(
        _dense_ffn_kernel,
        grid=(t // tm,),
        in_specs=[pl.BlockSpec((tm, d), lambda i: (i, 0)), _const_spec((1, d)),
                  _const_spec(wg.shape), _const_spec(wu.shape), _const_spec(wd.shape)],
        out_specs=pl.BlockSpec((tm, d), lambda i: (i, 0)),
        out_shape=jax.ShapeDtypeStruct((t, d), _F32),
        compiler_params=pltpu.CompilerParams(
            dimension_semantics=("arbitrary",), vmem_limit_bytes=VMEM_LIMIT_BYTES),
        name="dense_ffn",
    )(x, norm_g, wg, wu, wd)


def _router_kernel(x_ref, ng_ref, rw_ref, rb_ref, route_ref, *, n_experts):
    h = _rms_norm(x_ref[...], ng_ref[...]).astype(_BF16)
    logits = jnp.dot(h, rw_ref[...], preferred_element_type=_F32) + rb_ref[...]
    lane = lax.broadcasted_iota(jnp.int32, logits.shape, 1)
    neg = jnp.float32(-jnp.inf)
    logits = jnp.where(lane < n_experts, logits, neg)
    m1 = jnp.max(logits, axis=-1, keepdims=True)
    i1 = jnp.min(jnp.where(logits == m1, lane, LANES), axis=-1, keepdims=True)
    rest = jnp.where(lane == i1, neg, logits)
    m2 = jnp.max(rest, axis=-1, keepdims=True)
    i2 = jnp.min(jnp.where(rest == m2, lane, LANES), axis=-1, keepdims=True)
    e2 = jnp.exp(m2 - m1)
    den = 1.0 + e2
    route = jnp.where(lane == 0, i1.astype(_F32), 0.0)
    route = jnp.where(lane == 1, i2.astype(_F32), route)
    route = jnp.where(lane == 2, 1.0 / den, route)
    route_ref[...] = jnp.where(lane == 3, e2 / den, route)


def _router(x, norm_g, router_w, router_b, *, tm):
    t, d = x.shape
    n_experts = router_w.shape[1]
    assert t % tm == 0 and TOP_K <= n_experts <= LANES
    rw = jnp.zeros((d, LANES), _BF16).at[:, :n_experts].set(router_w.astype(_BF16))
    rb = jnp.zeros((1, LANES), _F32).at[0, :n_experts].set(router_b)
    return pl.pallas_call(
        functools.partial(_router_kernel, n_experts=n_experts),
        grid=(t // tm,),
        in_specs=[pl.BlockSpec((tm, d), lambda i: (i, 0)), _const_spec((1, d)),
                  _const_spec((d, LANES)), _const_spec((1, LANES))],
        out_specs=pl.BlockSpec((tm, LANES), lambda i: (i, 0)),
        out_shape=jax.ShapeDtypeStruct((t, LANES), _F32),
        compiler_params=pltpu.CompilerParams(
            dimension_semantics=("arbitrary",), vmem_limit_bytes=VMEM_LIMIT_BYTES),
        name="router",
    )(x, norm_g, rw, rb)


def _route_plan(i1, i2, n_experts, tg):
    t = i1.shape[0]
    e_pair = jnp.concatenate([i1, i2])
    ids = jnp.arange(n_experts, dtype=jnp.int32)
    onehot = (e_pair[:, None] == ids[None, :]).astype(jnp.int32)
    csum = jnp.cumsum(onehot, axis=0)
    rank = jnp.sum(csum * onehot, axis=1) - 1
    tiles_e = (csum[-1] + (tg - 1)) // tg
    tile_end = jnp.cumsum(tiles_e)
    start_row = (tile_end - tiles_e) * tg
    pos = jnp.sum(onehot * start_row[None, :], axis=1) + rank
    n_tiles = -(-2 * t // tg) + n_experts
    n_used = tile_end[-1]
    tile_ids = jnp.arange(n_tiles, dtype=jnp.int32)
    tile_e = jnp.sum((jnp.minimum(tile_ids, n_used - 1)[:, None] >= tile_end[None, :]).astype(jnp.int32), axis=1)
    return pos.reshape(2, t), tile_e.astype(jnp.int32), n_used.reshape(1).astype(jnp.int32)


def _row_copies(idx_ref, src_of, dst_of, sem, n, wait):
    def body(r, carry):
        for k in range(TOP_K):
            row = idx_ref[k * n + r]
            cp = pltpu.make_async_copy(src_of(k, r, row), dst_of(k, r, row), sem)
            if wait:
                cp.wait()
            else:
                cp.start()
        return carry
    lax.fori_loop(0, n, body, 0, unroll=8)


def _dispatch_kernel(pos_hbm, x_ref, xs_in_hbm, xs_hbm, idx_ref, sem_idx, sem_rows, *, tm):
    del xs_in_hbm
    cp = pltpu.make_async_copy(pos_hbm.at[pl.program_id(0)], idx_ref, sem_idx)
    cp.start()
    cp.wait()
    src_of = lambda k, r, row: x_ref.at[pl.ds(r, 1)]
    dst_of = lambda k, r, row: xs_hbm.at[pl.ds(row, 1)]
    _row_copies(idx_ref, src_of, dst_of, sem_rows, tm, wait=False)
    _row_copies(idx_ref, src_of, dst_of, sem_rows, tm, wait=True)


def _dispatch(x, pos_tiles, n_rows, *, tm):
    t, d = x.shape
    return pl.pallas_call(
        functools.partial(_dispatch_kernel, tm=tm),
        grid=(t // tm,),
        in_specs=[pl.BlockSpec(memory_space=pl.ANY), pl.BlockSpec((tm, d), lambda i: (i, 0)),
                  pl.BlockSpec(memory_space=pl.ANY)],
        out_specs=pl.BlockSpec(memory_space=pl.ANY),
        out_shape=jax.ShapeDtypeStruct((n_rows, d), _F32),
        scratch_shapes=[pltpu.SMEM((TOP_K * tm,), jnp.int32), pltpu.SemaphoreType.DMA,
                        pltpu.SemaphoreType.DMA],
        input_output_aliases={2: 0},
        compiler_params=pltpu.CompilerParams(dimension_semantics=("arbitrary",)),
        name="dispatch",
    )(pos_tiles, x, jnp.zeros((n_rows, d), _F32))


def _grouped_ffn_kernel(tile_e_ref, n_used_ref, xs_ref, ng_ref, wg_ref, wu_ref, wd_ref, ys_ref):
    del tile_e_ref

    @pl.when(pl.program_id(0) < n_used_ref[0])
    def _compute():
        h = _rms_norm(xs_ref[...], ng_ref[...]).astype(_BF16)
        ys_ref[...] = _swiglu(h, wg_ref[0], wu_ref[0], wd_ref[0])


def _grouped_ffn(xs, norm_g, tile_e, n_used, wg, wu, wd, *, tg):
    n_rows, d = xs.shape
    _, _, ff = wg.shape
    row_map = lambda i, te, nu: (jnp.minimum(i, nu[0] - 1), 0)
    w_map = lambda i, te, nu: (te[i], 0, 0)
    w_spec = lambda shape: pl.BlockSpec(shape, w_map, pipeline_mode=pl.Buffered(1))
    return pl.pallas_call(
        _grouped_ffn_kernel,
        grid_spec=pltpu.PrefetchScalarGridSpec(
            num_scalar_prefetch=2,
            grid=(n_rows // tg,),
            in_specs=[pl.BlockSpec((tg, d), row_map),
                      pl.BlockSpec((1, d), lambda i, te, nu: (0, 0), pipeline_mode=pl.Buffered(1)),
                      w_spec((1, d, ff)), w_spec((1, d, ff)), w_spec((1, ff, d))],
            out_specs=pl.BlockSpec((tg, d), row_map)),
        out_shape=jax.ShapeDtypeStruct((n_rows, d), _F32),
        compiler_params=pltpu.CompilerParams(
            dimension_semantics=("arbitrary",), vmem_limit_bytes=VMEM_LIMIT_BYTES),
        name="grouped_ffn",
    )(tile_e, n_used, xs, norm_g, wg, wu, wd)


def _combine_kernel(pos_hbm, x_ref, route_ref, ys_hbm, g_ref, o_ref, idx_ref, buf_ref, sem_idx, sem_rows,
                    *, tm, final_norm):
    i = pl.program_id(0)
    cp = pltpu.make_async_copy(pos_hbm.at[i], idx_ref, sem_idx)
    cp.start()
    cp.wait()
    src_of = lambda k, r, row: ys_hbm.at[pl.ds(row, 1)]
    dst_of = lambda k, r, row: buf_ref.at[k, pl.ds(r, 1)]
    _row_copies(idx_ref, src_of, dst_of, sem_rows, tm, wait=False)
    _row_copies(idx_ref, src_of, dst_of, sem_rows, tm, wait=True)
    route = route_ref[...]
    out = x_ref[...] + route[:, 2:3] * buf_ref[0] + route[:, 3:4] * buf_ref[1]
    if final_norm:
        out = _rms_norm(out, g_ref[...])
    o_ref[...] = out


def _combine(x, route, ys, pos_tiles, norm_g, *, tm, final_norm):
    t, d = x.shape
    return pl.pallas_call(
        functools.partial(_combine_kernel, tm=tm, final_norm=final_norm),
        grid=(t // tm,),
        in_specs=[pl.BlockSpec(memory_space=pl.ANY),
                  pl.BlockSpec((tm, d), lambda i: (i, 0)),
                  pl.BlockSpec((tm, LANES), lambda i: (i, 0)),
                  pl.BlockSpec(memory_space=pl.ANY),
                  _const_spec((1, d))],
        out_specs=pl.BlockSpec((tm, d), lambda i: (i, 0)),
        out_shape=jax.ShapeDtypeStruct((t, d), _F32),
        scratch_shapes=[pltpu.SMEM((TOP_K * tm,), jnp.int32), pltpu.VMEM((TOP_K, tm, d), _F32),
                        pltpu.SemaphoreType.DMA, pltpu.SemaphoreType.DMA],
        compiler_params=pltpu.CompilerParams(
            dimension_semantics=("arbitrary",), vmem_limit_bytes=VMEM_LIMIT_BYTES),
        name="combine",
    )(pos_tiles, x, route, ys, norm_g)


def _moe_ffn(x, lw, norm_final, *, tm, tg, tr):
    t, d = x.shape
    n_experts = lw['router_w'].shape[1]
    route = _router(x, lw['norm_ffn'], lw['router_w'], lw['router_b'], tm=tm)
    i1 = route[:, 0].astype(jnp.int32)
    i2 = route[:, 1].astype(jnp.int32)
    pos, tile_e, n_used = _route_plan(i1, i2, n_experts, tg)
    pos_tiles = pos.reshape(TOP_K, t // tr, tr).transpose(1, 0, 2).reshape(t // tr, TOP_K * tr)
    n_rows = tile_e.shape[0] * tg
    xs = _dispatch(x, pos_tiles, n_rows, tm=tr)
    ys = _grouped_ffn(xs, lw['norm_ffn'], tile_e, n_used, lw['wg'], lw['wu'], lw['wd'], tg=tg)
    g = lw['norm_ffn'] if norm_final is None else norm_final
    return _combine(x, route, ys, pos_tiles, g, tm=tr, final_norm=norm_final is not None)


def _final_norm_kernel(x_ref, g_ref, o_ref):
    o_ref[...] = _rms_norm(x_ref[...], g_ref[...])


def _final_norm(x, g, *, tm):
    t, d = x.shape
    assert t % tm == 0
    return pl.pallas_call(
        _final_norm_kernel,
        grid=(t // tm,),
        in_specs=[pl.BlockSpec((tm, d), lambda i: (i, 0)), _const_spec((1, d))],
        out_specs=pl.BlockSpec((tm, d), lambda i: (i, 0)),
        out_shape=jax.ShapeDtypeStruct((t, d), _F32),
        compiler_params=pltpu.CompilerParams(dimension_semantics=("arbitrary",)),
        name="final_norm",
    )(x, g)


def _trunk(x, conv_hist, pool_hist, pos0, layers, norm_final, *, tl, sb, tm, tg, tr):
    b, l, d = x.shape
    t = b * l
    tm, tg, tr = min(tm, t), min(tg, t), min(tr, t)
    new_conv, new_pool = [], []
    for li, lw in enumerate(layers):
        x, nc, npl = _mixer_layer(x, conv_hist[li], pool_hist[li], pos0, lw, tl=tl, sb=sb)
        new_conv.append(nc)
        new_pool.append(npl)
        xf = x.reshape(t, d)
        last = li == len(layers) - 1
        if 'router_w' in lw:
            xf = _moe_ffn(xf, lw, norm_final if last else None, tm=tm, tg=tg, tr=tr)
        else:
            xf = _dense_ffn(xf, lw['norm_ffn'], lw['wg'], lw['wu'], lw['wd'], tm=tm)
            if last:
                xf = _final_norm(xf, norm_final, tm=tm)
        x = xf.reshape(b, l, d)
    return x, jnp.stack(new_conv), jnp.stack(new_pool)


def kernel(x_prompt, x_sample, state_conv, state_pool, norm_mix, w_in, conv_w, conv_b, conv_ln_g, conv_ln_b, pool_w, pool_scale, w_out, norm_ffn, dense_w_gate, dense_w_up, dense_w_down, router_w, router_b, moe_w_gate, moe_w_up, moe_w_down, norm_final):
    depth = w_in.shape[0]
    row = lambda v: v.reshape(1, -1)
    layers = []
    for li in range(depth):
        lw = dict(
            norm_mix=row(norm_mix[li]), w_in=w_in[li].astype(_BF16), conv_w=conv_w[li],
            conv_b=row(conv_b[li]), conv_ln_g=row(conv_ln_g[li]), conv_ln_b=row(conv_ln_b[li]),
            pool_w=pool_w[li].astype(_BF16), pool_scale=row(pool_scale[li]),
            w_out=w_out[li].astype(_BF16), norm_ffn=row(norm_ffn[li]))
        j = li // 2
        if li % 2 == 0:
            lw.update(wg=dense_w_gate[j].astype(_BF16), wu=dense_w_up[j].astype(_BF16),
                      wd=dense_w_down[j].astype(_BF16))
        else:
            lw.update(router_w=router_w[j], router_b=router_b[j], wg=moe_w_gate[j].astype(_BF16),
                      wu=moe_w_up[j].astype(_BF16), wd=moe_w_down[j].astype(_BF16))
        layers.append(lw)
    nf = row(norm_final)

    bp = x_prompt.shape[0]
    zero_conv = jnp.zeros((depth, bp) + state_conv.shape[2:], x_prompt.dtype)
    zero_pool = jnp.zeros((depth, bp) + state_pool.shape[2:], x_prompt.dtype)
    y_p, conv_p, pool_p = _trunk(x_prompt, zero_conv, zero_pool, 0, layers, nf, **TILES)
    y_s, conv_s, pool_s = _trunk(x_sample, state_conv, state_pool, PAST_LEN, layers, nf, **TILES)
    return (y_p, y_s, conv_p, pool_p, conv_s, pool_s)
```

```python
import functools

import jax
import jax.numpy as jnp
from jax import lax
from jax.experimental import pallas as pl
from jax.experimental.pallas import tpu as pltpu

POOL_WINDOWS = (2, 4, 8, 16)
PAST_LEN = 2048
TOP_K = 2
EPS = 1e-6
LANES = 128
SUBLANES = 8
VMEM_LIMIT_BYTES = 56 * 1024 * 1024
TILES = dict(tl=512, sb=128, tm=512, tg=512, tr=1024)

_F32 = jnp.float32
_BF16 = jnp.bfloat16


def _round_up(n, m):
    return (n + m - 1) // m * m


def _rms_norm(x, g):
    ms = jnp.mean(x * x, axis=-1, keepdims=True)
    return x * lax.rsqrt(ms + EPS) * g


def _const_spec(shape):
    nd = len(shape)
    return pl.BlockSpec(shape, lambda *_: (0,) * nd, pipeline_mode=pl.Buffered(1))


def _mixer_kernel(x_ref, ch_ref, ph_ref, ng_ref, win_ref, cw_ref, cb_ref, lg_ref, lb_ref,
                  pw_ref, ps_ref, wout_ref,
                  y_ref, nc_ref, np_ref,
                  *scratch, pos0, tl, sb, rb, n_t):
    n_sb = tl // sb
    exta, extp = scratch[:n_sb], scratch[n_sb:2 * n_sb]
    h_ref, mix_ref, d_ref = scratch[2 * n_sb:]
    t = pl.program_id(1)
    n_cc = exta[0].shape[0]
    n_g = extp[0].shape[0]
    cc = n_cc * LANES
    k_taps = cw_ref.shape[0]
    kh = k_taps - 1
    ha = exta[0].shape[1] - sb
    ph = ph_ref.shape[1]
    hp = extp[0].shape[1] - sb

    def copy_history(dst_a, dst_p, src_a, src_p):
        for j in range(n_cc):
            dst_a[j, ha - kh:ha, :] = src_a[j, ha + sb - kh:ha + sb, :]
        for g in range(n_g):
            dst_p[g, hp - ph:hp, :] = src_p[g, hp + sb - ph:hp + sb, :]

    as_operand = lambda v: v.astype(_BF16).astype(_F32)

    @pl.when(t == 0)
    def _load_state():
        for j in range(n_cc):
            exta[0][j, ha - kh:ha, :] = as_operand(ch_ref[0, :, j * LANES:(j + 1) * LANES])
        for g in range(n_g):
            extp[0][g, hp - ph:hp, :] = ph_ref[0, :, g * LANES:(g + 1) * LANES]

    def project_in(s):
        ea, ep = exta[s], extp[s]
        if s > 0:
            copy_history(ea, ep, exta[s - 1], extp[s - 1])
        u = jnp.dot(h_ref[s * sb:(s + 1) * sb, :], win_ref[...], preferred_element_type=_F32)
        a = u[:, :cc] * jax.nn.sigmoid(u[:, cc:2 * cc])
        for j in range(n_cc):
            ea[j, ha:ha + sb, :] = as_operand(a[:, j * LANES:(j + 1) * LANES])
        for g in range(n_g):
            ep[g, hp:hp + sb, :] = u[:, 2 * cc + g * LANES:2 * cc + (g + 1) * LANES]
        if s == n_sb - 1:
            if sb >= kh:
                nc_ref[0] = a[sb - kh:, :]
            else:
                nc_ref[0, :kh - sb, :] = ch_ref[0, sb:, :]
                nc_ref[0, kh - sb:, :] = a

    def mix_block(s):
        s0 = s * sb
        rows = slice(s0, s0 + sb)
        ea, ep = exta[s], extp[s]
        for r0 in range(0, sb, rb):
            ys = []
            for j in range(n_cc):
                ls = slice(j * LANES, (j + 1) * LANES)
                acc = jnp.zeros((rb, LANES), _F32)
                for k in range(k_taps):
                    e0 = r0 + ha - kh + k
                    acc = acc + cw_ref[k:k + 1, ls] * ea[j, e0:e0 + rb, :]
                ys.append(acc + cb_ref[:, ls])
            tot = ys[0]
            for y in ys[1:]:
                tot = tot + y
            mu = jnp.sum(tot, axis=-1, keepdims=True) * (1.0 / cc)
            xc = [y - mu for y in ys]
            sq = xc[0] * xc[0]
            for c in xc[1:]:
                sq = sq + c * c
            inv = lax.rsqrt(jnp.sum(sq, axis=-1, keepdims=True) * (1.0 / cc) + EPS)
            for j in range(n_cc):
                ls = slice(j * LANES, (j + 1) * LANES)
                v = xc[j] * inv * lg_ref[:, ls] + lb_ref[:, ls]
                mix_ref[s0 + r0:s0 + r0 + rb, ls] = (v * jax.nn.sigmoid(v)).astype(_BF16)
            row = lax.broadcasted_iota(jnp.int32, (rb, 1), 0)
            pos1 = (row + (pos0 + 1 + t * tl + s0 + r0)).astype(_F32)
            for g, w in enumerate(POOL_WINDOWS):
                cur = ep[g, hp + r0:hp + r0 + rb, :]
                acc = cur
                for back in range(1, w):
                    acc = acc + ep[g, hp + r0 - back:hp + r0 - back + rb, :]
                d = acc / jnp.minimum(pos1, float(w)) - cur
                d_ref[s0 + r0:s0 + r0 + rb, g * LANES:(g + 1) * LANES] = d.astype(_BF16)

        for g in range(n_g):
            ls = slice(g * LANES, (g + 1) * LANES)
            z = jnp.dot(d_ref[rows, ls], pw_ref[g], preferred_element_type=_F32) * ps_ref[:, ls]
            mix_ref[rows, cc + g * LANES:cc + (g + 1) * LANES] = z.astype(_BF16)
        y_ref[0, rows, :] = x_ref[0, rows, :] + jnp.dot(mix_ref[rows, :], wout_ref[...],
                                                        preferred_element_type=_F32)

    h_ref[...] = _rms_norm(x_ref[0], ng_ref[...]).astype(_BF16)
    project_in(0)
    for s in range(n_sb):
        if s + 1 < n_sb:
            project_in(s + 1)
        mix_block(s)

    last_a, last_p = exta[n_sb - 1], extp[n_sb - 1]
    for g in range(n_g):
        np_ref[0, :, g * LANES:(g + 1) * LANES] = last_p[g, hp + sb - ph:hp + sb, :]
    if n_t > 1:
        copy_history(exta[0], extp[0], last_a, last_p)


def _mixer_layer(x, conv_hist, pool_hist, pos0, w, *, tl, sb):
    b, l, d = x.shape
    kh, cc = conv_hist.shape[1], conv_hist.shape[2]
    ph, pc = pool_hist.shape[1], pool_hist.shape[2]
    n_g = len(POOL_WINDOWS)
    assert cc % LANES == 0 and pc == n_g * LANES, (cc, pc)
    assert ph >= max(POOL_WINDOWS) - 1 and w['conv_w'].shape[0] == kh + 1
    tl = min(tl, l)
    n_t = l // tl
    sb = min(sb, tl)
    assert l % tl == 0 and (l == sb or sb >= max(kh, ph))
    rb = min(16, sb)
    assert tl % sb == 0 and sb % rb == 0 and rb % 16 == 0
    ha, hp = _round_up(kh, SUBLANES), _round_up(ph, SUBLANES)

    kern = functools.partial(_mixer_kernel, pos0=pos0, tl=tl, sb=sb, rb=rb, n_t=n_t)
    seq_spec = lambda rows, ch: pl.BlockSpec((1, rows, ch), lambda i, j: (i, 0, 0))
    return pl.pallas_call(
        kern,
        grid=(b, n_t),
        in_specs=[
            pl.BlockSpec((1, tl, d), lambda i, j: (i, j, 0)),
            seq_spec(kh, cc), seq_spec(ph, pc),
            _const_spec((1, d)), _const_spec(w['w_in'].shape),
            _const_spec(w['conv_w'].shape), _const_spec((1, cc)), _const_spec((1, cc)), _const_spec((1, cc)),
            _const_spec(w['pool_w'].shape), _const_spec((1, pc)), _const_spec(w['w_out'].shape),
        ],
        out_specs=[
            pl.BlockSpec((1, tl, d), lambda i, j: (i, j, 0)),
            seq_spec(kh, cc), seq_spec(ph, pc),
        ],
        out_shape=[
            jax.ShapeDtypeStruct((b, l, d), _F32),
            jax.ShapeDtypeStruct((b, kh, cc), _F32),
            jax.ShapeDtypeStruct((b, ph, pc), _F32),
        ],
        scratch_shapes=(
            [pltpu.VMEM((cc // LANES, ha + sb, LANES), _F32)] * (tl // sb)
            + [pltpu.VMEM((n_g, hp + sb, LANES), _F32)] * (tl // sb)
            + [pltpu.VMEM((tl, d), _BF16), pltpu.VMEM((tl, cc + pc), _BF16), pltpu.VMEM((tl, pc), _BF16)]),
        compiler_params=pltpu.CompilerParams(
            dimension_semantics=("arbitrary", "arbitrary"), vmem_limit_bytes=VMEM_LIMIT_BYTES),
        name="mixer",
    )(x, conv_hist, pool_hist, w['norm_mix'], w['w_in'], w['conv_w'], w['conv_b'], w['conv_ln_g'],
      w['conv_ln_b'], w['pool_w'], w['pool_scale'], w['w_out'])


def _swiglu(h, wg, wu, wd):
    g = jnp.dot(h, wg, preferred_element_type=_F32)
    u = jnp.dot(h, wu, preferred_element_type=_F32)
    act = (g * jax.nn.sigmoid(g) * u).astype(_BF16)
    return jnp.dot(act, wd, preferred_element_type=_F32)


def _dense_ffn_kernel(x_ref, ng_ref, wg_ref, wu_ref, wd_ref, o_ref):
    x = x_ref[...]
    h = _rms_norm(x, ng_ref[...]).astype(_BF16)
    o_ref[...] = x + _swiglu(h, wg_ref[...], wu_ref[...], wd_ref[...])


def _dense_ffn(x, norm_g, wg, wu, wd, *, tm):
    t, d = x.shape
    assert t % tm == 0
    return pl.pallas_call(
        _dense_ffn_kernel,
        grid=(t // tm,),
        in_specs=[pl.BlockSpec((tm, d), lambda i: (i, 0)), _const_spec((1, d)),
                  _const_spec(wg.shape), _const_spec(wu.shape), _const_spec(wd.shape)],
        out_specs=pl.BlockSpec((tm, d), lambda i: (i, 0)),
        out_shape=jax.ShapeDtypeStruct((t, d), _F32),
        compiler_params=pltpu.CompilerParams(
            dimension_semantics=("arbitrary",), vmem_limit_bytes=VMEM_LIMIT_BYTES),
        name="dense_ffn",
    )(x, norm_g, wg, wu, wd)


def _router_kernel(x_ref, ng_ref, rw_ref, rb_ref, route_ref, *, n_experts):
    h = _rms_norm(x_ref[...], ng_ref[...]).astype(_BF16)
    logits = jnp.dot(h, rw_ref[...], preferred_element_type=_F32) + rb_ref[...]
    lane = lax.broadcasted_iota(jnp.int32, logits.shape, 1)
    neg = jnp.float32(-jnp.inf)
    logits = jnp.where(lane < n_experts, logits, neg)
    m1 = jnp.max(logits, axis=-1, keepdims=True)
    i1 = jnp.min(jnp.where(logits == m1, lane, LANES), axis=-1, keepdims=True)
    rest = jnp.where(lane == i1, neg, logits)
    m2 = jnp.max(rest, axis=-1, keepdims=True)
    i2 = jnp.min(jnp.where(rest == m2, lane, LANES), axis=-1, keepdims=True)
    e2 = jnp.exp(m2 - m1)
    den = 1.0 + e2
    route = jnp.where(lane == 0, i1.astype(_F32), 0.0)
    route = jnp.where(lane == 1, i2.astype(_F32), route)
    route = jnp.where(lane == 2, 1.0 / den, route)
    route_ref[...] = jnp.where(lane == 3, e2 / den, route)


def _router(x, norm_g, router_w, router_b, *, tm):
    t, d = x.shape
    n_experts = router_w.shape[1]
    assert t % tm == 0 and TOP_K <= n_experts <= LANES
    rw = jnp.zeros((d, LANES), _BF16).at[:, :n_experts].set(router_w.astype(_BF16))
    rb = jnp.zeros((1, LANES), _F32).at[0, :n_experts].set(router_b)
    return pl.pallas_call(
        functools.partial(_router_kernel, n_experts=n_experts),
        grid=(t // tm,),
        in_specs=[pl.BlockSpec((tm, d), lambda i: (i, 0)), _const_spec((1, d)),
                  _const_spec((d, LANES)), _const_spec((1, LANES))],
        out_specs=pl.BlockSpec((tm, LANES), lambda i: (i, 0)),
        out_shape=jax.ShapeDtypeStruct((t, LANES), _F32),
        compiler_params=pltpu.CompilerParams(
            dimension_semantics=("arbitrary",), vmem_limit_bytes=VMEM_LIMIT_BYTES),
        name="router",
    )(x, norm_g, rw, rb)


def _route_plan(i1, i2, n_experts, tg):
    t = i1.shape[0]
    e_pair = jnp.concatenate([i1, i2])
    ids = jnp.arange(n_experts, dtype=jnp.int32)
    onehot = (e_pair[:, None] == ids[None, :]).astype(jnp.int32)
    csum = jnp.cumsum(onehot, axis=0)
    rank = jnp.sum(csum * onehot, axis=1) - 1
    counts = csum[-1]
    tiles_e = (counts + (tg - 1)) // tg
    tile_end = jnp.cumsum(tiles_e)
    start_row = (tile_end - tiles_e) * tg
    pos = jnp.sum(onehot * start_row[None, :], axis=1) + rank
    n_tiles = -(-2 * t // tg) + n_experts
    n_used = tile_end[-1]
    tile_ids = jnp.arange(n_tiles, dtype=jnp.int32)
    tile_e = jnp.sum((jnp.minimum(tile_ids, n_used - 1)[:, None] >= tile_end[None, :]).astype(jnp.int32), axis=1)
    pad_start = (start_row + counts).astype(jnp.int32)
    pad_len = (tiles_e * tg - counts).astype(jnp.int32)
    return pos.reshape(2, t), tile_e.astype(jnp.int32), n_used.reshape(1).astype(jnp.int32), pad_start, pad_len


def _row_copies(idx_ref, src_of, dst_of, sem, n, wait):
    def body(i, carry):
        for j in range(SUBLANES):
            for k in range(TOP_K):
                row = idx_ref[k * n + i * SUBLANES + j]
                cp = pltpu.make_async_copy(src_of(k, i, j, row), dst_of(k, i, j, row), sem)
                if wait:
                    cp.wait()
                else:
                    cp.start()
        return carry
    lax.fori_loop(0, n // SUBLANES, body, 0)


def _dispatch_kernel(pad_start_ref, pad_len_ref, pos_hbm, x_ref, xs_hbm, idx_ref, zero_ref, sem_idx, sem_rows,
                     *, tm):
    @pl.when(pl.program_id(0) == 0)
    def _zero_padding_rows():
        zero_ref[...] = jnp.zeros_like(zero_ref)
        for wait in (False, True):
            for e in range(pad_start_ref.shape[0]):
                def body(j, carry, e=e, wait=wait):
                    cp = pltpu.make_async_copy(zero_ref, xs_hbm.at[pl.ds(pad_start_ref[e] + j, 1)], sem_rows)
                    if wait:
                        cp.wait()
                    else:
                        cp.start()
                    return carry
                lax.fori_loop(0, pad_len_ref[e], body, 0)

    cp = pltpu.make_async_copy(pos_hbm.at[pl.program_id(0)], idx_ref, sem_idx)
    cp.start()
    cp.wait()
    src_of = lambda k, i, j, row: x_ref.at[i, pl.ds(j, 1)]
    dst_of = lambda k, i, j, row: xs_hbm.at[pl.ds(row, 1)]
    _row_copies(idx_ref, src_of, dst_of, sem_rows, tm, wait=False)
    _row_copies(idx_ref, src_of, dst_of, sem_rows, tm, wait=True)


def _dispatch(x, pos_tiles, pad_start, pad_len, n_rows, *, tm):
    t, d = x.shape
    assert tm % SUBLANES == 0
    return pl.pallas_call(
        functools.partial(_dispatch_kernel, tm=tm),
        grid_spec=pltpu.PrefetchScalarGridSpec(
            num_scalar_prefetch=2,
            grid=(t // tm,),
            in_specs=[pl.BlockSpec(memory_space=pl.ANY),
                      pl.BlockSpec((tm // SUBLANES, SUBLANES, d), lambda i, ps, pn: (i, 0, 0))],
            out_specs=pl.BlockSpec(memory_space=pl.ANY),
            scratch_shapes=[pltpu.SMEM((TOP_K * tm,), jnp.int32), pltpu.VMEM((1, d), _F32),
                            pltpu.SemaphoreType.DMA, pltpu.SemaphoreType.DMA]),
        out_shape=jax.ShapeDtypeStruct((n_rows, d), _F32),
        compiler_params=pltpu.CompilerParams(dimension_semantics=("arbitrary",)),
        name="dispatch",
    )(pad_start, pad_len, pos_tiles, x.reshape(t // SUBLANES, SUBLANES, d))


def _grouped_ffn_kernel(tile_e_ref, n_used_ref, xs_ref, ng_ref, wg_ref, wu_ref, wd_ref, ys_ref):
    del tile_e_ref

    @pl.when(pl.program_id(0) < n_used_ref[0])
    def _compute():
        h = _rms_norm(xs_ref[...], ng_ref[...]).astype(_BF16)
        ys_ref[...] = _swiglu(h, wg_ref[0], wu_ref[0], wd_ref[0])


def _grouped_ffn(xs, norm_g, tile_e, n_used, wg, wu, wd, *, tg):
    n_rows, d = xs.shape
    _, _, ff = wg.shape
    row_map = lambda i, te, nu: (jnp.minimum(i, nu[0] - 1), 0)
    w_map = lambda i, te, nu: (te[i], 0, 0)
    w_spec = lambda shape: pl.BlockSpec(shape, w_map, pipeline_mode=pl.Buffered(1))
    return pl.pallas_call(
        _grouped_ffn_kernel,
        grid_spec=pltpu.PrefetchScalarGridSpec(
            num_scalar_prefetch=2,
            grid=(n_rows // tg,),
            in_specs=[pl.BlockSpec((tg, d), row_map),
                      pl.BlockSpec((1, d), lambda i, te, nu: (0, 0), pipeline_mode=pl.Buffered(1)),
                      w_spec((1, d, ff)), w_spec((1, d, ff)), w_spec((1, ff, d))],
            out_specs=pl.BlockSpec((tg, d), row_map)),
        out_shape=jax.ShapeDtypeStruct((n_rows, d), _F32),
        compiler_params=pltpu.CompilerParams(
            dimension_semantics=("arbitrary",), vmem_limit_bytes=VMEM_LIMIT_BYTES),
        name="grouped_ffn",
    )(tile_e, n_used, xs, norm_g, wg, wu, wd)


def _combine_kernel(pos_hbm, x_ref, route_ref, ys_hbm, g_ref, o_ref, idx_ref, buf_ref, sem_idx, sem_rows,
                    *, tm, final_norm):
    i = pl.program_id(0)
    cp = pltpu.make_async_copy(pos_hbm.at[i], idx_ref, sem_idx)
    cp.start()
    cp.wait()
    src_of = lambda k, i, j, row: ys_hbm.at[pl.ds(row, 1)]
    dst_of = lambda k, i, j, row: buf_ref.at[k, i, pl.ds(j, 1)]
    _row_copies(idx_ref, src_of, dst_of, sem_rows, tm, wait=False)
    _row_copies(idx_ref, src_of, dst_of, sem_rows, tm, wait=True)
    route = route_ref[...]
    y1 = buf_ref[0].reshape(x_ref.shape)
    y2 = buf_ref[1].reshape(x_ref.shape)
    out = x_ref[...] + route[:, 2:3] * y1 + route[:, 3:4] * y2
    if final_norm:
        out = _rms_norm(out, g_ref[...])
    o_ref[...] = out


def _combine(x, route, ys, pos_tiles, norm_g, *, tm, final_norm):
    t, d = x.shape
    return pl.pallas_call(
        functools.partial(_combine_kernel, tm=tm, final_norm=final_norm),
        grid=(t // tm,),
        in_specs=[pl.BlockSpec(memory_space=pl.ANY),
                  pl.BlockSpec((tm, d), lambda i: (i, 0)),
                  pl.BlockSpec((tm, LANES), lambda i: (i, 0)),
                  pl.BlockSpec(memory_space=pl.ANY),
                  _const_spec((1, d))],
        out_specs=pl.BlockSpec((tm, d), lambda i: (i, 0)),
        out_shape=jax.ShapeDtypeStruct((t, d), _F32),
        scratch_shapes=[pltpu.SMEM((TOP_K * tm,), jnp.int32),
                        pltpu.VMEM((TOP_K, tm // SUBLANES, SUBLANES, d), _F32),
                        pltpu.SemaphoreType.DMA, pltpu.SemaphoreType.DMA],
        compiler_params=pltpu.CompilerParams(
            dimension_semantics=("arbitrary",), vmem_limit_bytes=VMEM_LIMIT_BYTES),
        name="combine",
    )(pos_tiles, x, route, ys, norm_g)


def _moe_ffn(x, lw, norm_final, *, tm, tg, tr):
    t, d = x.shape
    n_experts = lw['router_w'].shape[1]
    route = _router(x, lw['norm_ffn'], lw['router_w'], lw['router_b'], tm=tm)
    i1 = route[:, 0].astype(jnp.int32)
    i2 = route[:, 1].astype(jnp.int32)
    pos, tile_e, n_used, pad_start, pad_len = _route_plan(i1, i2, n_experts, tg)
    pos_tiles = pos.reshape(TOP_K, t // tr, tr).transpose(1, 0, 2).reshape(t // tr, TOP_K * tr)
    n_rows = tile_e.shape[0] * tg
    xs = _dispatch(x, pos_tiles, pad_start, pad_len, n_rows, tm=tr)
    ys = _grouped_ffn(xs, lw['norm_ffn'], tile_e, n_used, lw['wg'], lw['wu'], lw['wd'], tg=tg)
    g = lw['norm_ffn'] if norm_final is None else norm_final
    return _combine(x, route, ys, pos_tiles, g, tm=tr, final_norm=norm_final is not None)


def _final_norm_kernel(x_ref, g_ref, o_ref):
    o_ref[...] = _rms_norm(x_ref[...], g_ref[...])


def _final_norm(x, g, *, tm):
    t, d = x.shape
    assert t % tm == 0
    return pl.pallas_call(
        _final_norm_kernel,
        grid=(t // tm,),
        in_specs=[pl.BlockSpec((tm, d), lambda i: (i, 0)), _const_spec((1, d))],
        out_specs=pl.BlockSpec((tm, d), lambda i: (i, 0)),
        out_shape=jax.ShapeDtypeStruct((t, d), _F32),
        compiler_params=pltpu.CompilerParams(dimension_semantics=("arbitrary",)),
        name="final_norm",
    )(x, g)


def _trunk(x, conv_hist, pool_hist, pos0, layers, norm_final, *, tl, sb, tm, tg, tr):
    b, l, d = x.shape
    t = b * l
    tm, tg, tr = min(tm, t), min(tg, t), min(tr, t)
    new_conv, new_pool = [], []
    for li, lw in enumerate(layers):
        x, nc, npl = _mixer_layer(x, conv_hist[li], pool_hist[li], pos0, lw, tl=tl, sb=sb)
        new_conv.append(nc)
        new_pool.append(npl)
        xf = x.reshape(t, d)
        last = li == len(layers) - 1
        if 'router_w' in lw:
            xf = _moe_ffn(xf, lw, norm_final if last else None, tm=tm, tg=tg, tr=tr)
        else:
            xf = _dense_ffn(xf, lw['norm_ffn'], lw['wg'], lw['wu'], lw['wd'], tm=tm)
            if last:
                xf = _final_norm(xf, norm_final, tm=tm)
        x = xf.reshape(b, l, d)
    return x, jnp.stack(new_conv), jnp.stack(new_pool)


def kernel(x_prompt, x_sample, state_conv, state_pool, norm_mix, w_in, conv_w, conv_b, conv_ln_g, conv_ln_b, pool_w, pool_scale, w_out, norm_ffn, dense_w_gate, dense_w_up, dense_w_down, router_w, router_b, moe_w_gate, moe_w_up, moe_w_down, norm_final):
    depth = w_in.shape[0]
    row = lambda v: v.reshape(1, -1)
    layers = []
    for li in range(depth):
        lw = dict(
            norm_mix=row(norm_mix[li]), w_in=w_in[li].astype(_BF16),
            conv_w=conv_w[li].astype(_BF16).astype(_F32),
            conv_b=row(conv_b[li]), conv_ln_g=row(conv_ln_g[li]), conv_ln_b=row(conv_ln_b[li]),
            pool_w=pool_w[li].astype(_BF16), pool_scale=row(pool_scale[li]),
            w_out=w_out[li].astype(_BF16), norm_ffn=row(norm_ffn[li]))
        j = li // 2
        if li % 2 == 0:
            lw.update(wg=dense_w_gate[j].astype(_BF16), wu=dense_w_up[j].astype(_BF16),
                      wd=dense_w_down[j].astype(_BF16))
        else:
            lw.update(router_w=router_w[j], router_b=router_b[j], wg=moe_w_gate[j].astype(_BF16),
                      wu=moe_w_up[j].astype(_BF16), wd=moe_w_down[j].astype(_BF16))
        layers.append(lw)
    nf = row(norm_final)

    bp = x_prompt.shape[0]
    zero_conv = jnp.zeros((depth, bp) + state_conv.shape[2:], x_prompt.dtype)
    zero_pool = jnp.zeros((depth, bp) + state_pool.shape[2:], x_prompt.dtype)
    y_p, conv_p, pool_p = _trunk(x_prompt, zero_conv, zero_pool, 0, layers, nf, **TILES)
    y_s, conv_s, pool_s = _trunk(x_sample, state_conv, state_pool, PAST_LEN, layers, nf, **TILES)
    return (y_p, y_s, conv_p, pool_p, conv_s, pool_s)
```

```python
import functools

import jax
import jax.numpy as jnp
from jax import lax
from jax.experimental import pallas as pl
from jax.experimental.pallas import tpu as pltpu

POOL_WINDOWS = (2, 4, 8, 16)
PAST_LEN = 2048
TOP_K = 2
EPS = 1e-6
LANES = 128
SUBLANES = 8
VMEM_LIMIT_BYTES = 56 * 1024 * 1024
TILES = dict(tl=512, sb=128, tm=512, tg=512, td=4096, tc=1024)

_F32 = jnp.float32
_BF16 = jnp.bfloat16


def _round_up(n, m):
    return (n + m - 1) // m * m


def _rms_norm(x, g):
    ms = jnp.mean(x * x, axis=-1, keepdims=True)
    return x * lax.rsqrt(ms + EPS) * g


def _dot_f32_lhs(h, w):
    return lax.dot_general(h, w, (((1,), (0,)), ((), ())), preferred_element_type=_F32)


def _const_spec(shape):
    nd = len(shape)
    return pl.BlockSpec(shape, lambda *_: (0,) * nd, pipeline_mode=pl.Buffered(1))


def _mixer_kernel(x_ref, ch_ref, ph_ref, ng_ref, win_ref, cw_ref, cb_ref, lg_ref, lb_ref,
                  pw_ref, ps_ref, wout_ref,
                  y_ref, nc_ref, np_ref,
                  *scratch, pos0, tl, sb, rb, n_t):
    n_sb = tl // sb
    exta, extp = scratch[:n_sb], scratch[n_sb:2 * n_sb]
    h_ref, mix_ref, d_ref = scratch[2 * n_sb:]
    t = pl.program_id(1)
    n_cc = exta[0].shape[0]
    n_g = extp[0].shape[0]
    cc = n_cc * LANES
    k_taps = cw_ref.shape[0]
    kh = k_taps - 1
    ha = exta[0].shape[1] - sb
    ph = ph_ref.shape[1]
    hp = extp[0].shape[1] - sb

    def copy_history(dst_a, dst_p, src_a, src_p):
        for j in range(n_cc):
            dst_a[j, ha - kh:ha, :] = src_a[j, ha + sb - kh:ha + sb, :]
        for g in range(n_g):
            dst_p[g, hp - ph:hp, :] = src_p[g, hp + sb - ph:hp + sb, :]

    as_operand = lambda v: v.astype(_BF16).astype(_F32)

    @pl.when(t == 0)
    def _load_state():
        for j in range(n_cc):
            exta[0][j, ha - kh:ha, :] = as_operand(ch_ref[0, :, j * LANES:(j + 1) * LANES])
        for g in range(n_g):
            extp[0][g, hp - ph:hp, :] = ph_ref[0, :, g * LANES:(g + 1) * LANES]

    def project_in(s):
        ea, ep = exta[s], extp[s]
        if s > 0:
            copy_history(ea, ep, exta[s - 1], extp[s - 1])
        u = _dot_f32_lhs(h_ref[s * sb:(s + 1) * sb, :], win_ref[...])
        a = u[:, :cc] * jax.nn.sigmoid(u[:, cc:2 * cc])
        for j in range(n_cc):
            ea[j, ha:ha + sb, :] = as_operand(a[:, j * LANES:(j + 1) * LANES])
        for g in range(n_g):
            ep[g, hp:hp + sb, :] = u[:, 2 * cc + g * LANES:2 * cc + (g + 1) * LANES]
        if s == n_sb - 1:
            if sb >= kh:
                nc_ref[0] = a[sb - kh:, :]
            else:
                nc_ref[0, :kh - sb, :] = ch_ref[0, sb:, :]
                nc_ref[0, kh - sb:, :] = a

    def mix_block(s):
        s0 = s * sb
        rows = slice(s0, s0 + sb)
        ea, ep = exta[s], extp[s]
        for r0 in range(0, sb, rb):
            ys = []
            for j in range(n_cc):
                ls = slice(j * LANES, (j + 1) * LANES)
                acc = jnp.zeros((rb, LANES), _F32)
                for k in range(k_taps):
                    e0 = r0 + ha - kh + k
                    acc = acc + cw_ref[k:k + 1, ls] * ea[j, e0:e0 + rb, :]
                ys.append(acc + cb_ref[:, ls])
            tot = ys[0]
            for y in ys[1:]:
                tot = tot + y
            mu = jnp.sum(tot, axis=-1, keepdims=True) * (1.0 / cc)
            xc = [y - mu for y in ys]
            sq = xc[0] * xc[0]
            for c in xc[1:]:
                sq = sq + c * c
            inv = lax.rsqrt(jnp.sum(sq, axis=-1, keepdims=True) * (1.0 / cc) + EPS)
            for j in range(n_cc):
                ls = slice(j * LANES, (j + 1) * LANES)
                v = xc[j] * inv * lg_ref[:, ls] + lb_ref[:, ls]
                mix_ref[s0 + r0:s0 + r0 + rb, ls] = (v * jax.nn.sigmoid(v)).astype(_BF16)
            row = lax.broadcasted_iota(jnp.int32, (rb, 1), 0)
            pos1 = (row + (pos0 + 1 + t * tl + s0 + r0)).astype(_F32)
            for g, w in enumerate(POOL_WINDOWS):
                cur = ep[g, hp + r0:hp + r0 + rb, :]
                acc = cur
                for back in range(1, w):
                    acc = acc + ep[g, hp + r0 - back:hp + r0 - back + rb, :]
                d = acc / jnp.minimum(pos1, float(w)) - cur
                d_ref[s0 + r0:s0 + r0 + rb, g * LANES:(g + 1) * LANES] = d.astype(_BF16)

        for g in range(n_g):
            ls = slice(g * LANES, (g + 1) * LANES)
            z = jnp.dot(d_ref[rows, ls], pw_ref[g], preferred_element_type=_F32) * ps_ref[:, ls]
            mix_ref[rows, cc + g * LANES:cc + (g + 1) * LANES] = z.astype(_BF16)
        y_ref[0, rows, :] = x_ref[0, rows, :] + jnp.dot(mix_ref[rows, :], wout_ref[...],
                                                        preferred_element_type=_F32)

    h_ref[...] = _rms_norm(x_ref[0], ng_ref[...])
    project_in(0)
    for s in range(n_sb):
        if s + 1 < n_sb:
            project_in(s + 1)
        mix_block(s)

    last_a, last_p = exta[n_sb - 1], extp[n_sb - 1]
    for g in range(n_g):
        np_ref[0, :, g * LANES:(g + 1) * LANES] = last_p[g, hp + sb - ph:hp + sb, :]
    if n_t > 1:
        copy_history(exta[0], extp[0], last_a, last_p)


def _mixer_layer(x, conv_hist, pool_hist, pos0, w, *, tl, sb):
    b, l, d = x.shape
    kh, cc = conv_hist.shape[1], conv_hist.shape[2]
    ph, pc = pool_hist.shape[1], pool_hist.shape[2]
    n_g = len(POOL_WINDOWS)
    assert cc % LANES == 0 and pc == n_g * LANES, (cc, pc)
    assert ph >= max(POOL_WINDOWS) - 1 and w['conv_w'].shape[0] == kh + 1
    tl = min(tl, l)
    n_t = l // tl
    sb = min(sb, tl)
    assert l % tl == 0 and (l == sb or sb >= max(kh, ph))
    rb = min(16, sb)
    assert tl % sb == 0 and sb % rb == 0 and rb % 16 == 0
    ha, hp = _round_up(kh, SUBLANES), _round_up(ph, SUBLANES)

    kern = functools.partial(_mixer_kernel, pos0=pos0, tl=tl, sb=sb, rb=rb, n_t=n_t)
    seq_spec = lambda rows, ch: pl.BlockSpec((1, rows, ch), lambda i, j: (i, 0, 0))
    return pl.pallas_call(
        kern,
        grid=(b, n_t),
        in_specs=[
            pl.BlockSpec((1, tl, d), lambda i, j: (i, j, 0)),
            seq_spec(kh, cc), seq_spec(ph, pc),
            _const_spec((1, d)), _const_spec(w['w_in'].shape),
            _const_spec(w['conv_w'].shape), _const_spec((1, cc)), _const_spec((1, cc)), _const_spec((1, cc)),
            _const_spec(w['pool_w'].shape), _const_spec((1, pc)), _const_spec(w['w_out'].shape),
        ],
        out_specs=[
            pl.BlockSpec((1, tl, d), lambda i, j: (i, j, 0)),
            seq_spec(kh, cc), seq_spec(ph, pc),
        ],
        out_shape=[
            jax.ShapeDtypeStruct((b, l, d), _F32),
            jax.ShapeDtypeStruct((b, kh, cc), _F32),
            jax.ShapeDtypeStruct((b, ph, pc), _F32),
        ],
        scratch_shapes=(
            [pltpu.VMEM((cc // LANES, ha + sb, LANES), _F32)] * (tl // sb)
            + [pltpu.VMEM((n_g, hp + sb, LANES), _F32)] * (tl // sb)
            + [pltpu.VMEM((tl, d), _F32), pltpu.VMEM((tl, cc + pc), _BF16), pltpu.VMEM((tl, pc), _BF16)]),
        compiler_params=pltpu.CompilerParams(
            dimension_semantics=("arbitrary", "arbitrary"), vmem_limit_bytes=VMEM_LIMIT_BYTES),
        name="mixer",
    )(x, conv_hist, pool_hist, w['norm_mix'], w['w_in'], w['conv_w'], w['conv_b'], w['conv_ln_g'],
      w['conv_ln_b'], w['pool_w'], w['pool_scale'], w['w_out'])


def _swiglu(h, wg, wu, wd):
    g = _dot_f32_lhs(h, wg)
    u = _dot_f32_lhs(h, wu)
    act = (g * jax.nn.sigmoid(g) * u).astype(_BF16)
    return jnp.dot(act, wd, preferred_element_type=_F32)


def _dense_ffn_kernel(x_ref, ng_ref, wg_ref, wu_ref, wd_ref, o_ref):
    x = x_ref[...]
    h = _rms_norm(x, ng_ref[...])
    o_ref[...] = x + _swiglu(h, wg_ref[...], wu_ref[...], wd_ref[...])


def _dense_ffn(x, norm_g, wg, wu, wd, *, tm):
    t, d = x.shape
    assert t % tm == 0
    return pl.pallas_call(
        _dense_ffn_kernel,
        grid=(t // tm,),
        in_specs=[pl.BlockSpec((tm, d), lambda i: (i, 0)), _const_spec((1, d)),
                  _const_spec(wg.shape), _const_spec(wu.shape), _const_spec(wd.shape)],
        out_specs=pl.BlockSpec((tm, d), lambda i: (i, 0)),
        out_shape=jax.ShapeDtypeStruct((t, d), _F32),
        compiler_params=pltpu.CompilerParams(
            dimension_semantics=("arbitrary",), vmem_limit_bytes=VMEM_LIMIT_BYTES),
        name="dense_ffn",
    )(x, norm_g, wg, wu, wd)


def _router_kernel(x_ref, ng_ref, rw_ref, rb_ref, route_ref, *, n_experts):
    h = _rms_norm(x_ref[...], ng_ref[...])
    logits = _dot_f32_lhs(h, rw_ref[...]) + rb_ref[...]
    lane = lax.broadcasted_iota(jnp.int32, logits.shape, 1)
    neg = jnp.float32(-jnp.inf)
    logits = jnp.where(lane < n_experts, logits, neg)
    m1 = jnp.max(logits, axis=-1, keepdims=True)
    i1 = jnp.min(jnp.where(logits == m1, lane, LANES), axis=-1, keepdims=True)
    rest = jnp.where(lane == i1, neg, logits)
    m2 = jnp.max(rest, axis=-1, keepdims=True)
    i2 = jnp.min(jnp.where(rest == m2, lane, LANES), axis=-1, keepdims=True)
    e2 = jnp.exp(m2 - m1)
    den = 1.0 + e2
    route = jnp.where(lane == 0, i1.astype(_F32), 0.0)
    route = jnp.where(lane == 1, i2.astype(_F32), route)
    route = jnp.where(lane == 2, 1.0 / den, route)
    route_ref[...] = jnp.where(lane == 3, e2 / den, route)


def _router(x, norm_g, router_w, router_b, *, tm):
    t, d = x.shape
    n_experts = router_w.shape[1]
    assert t % tm == 0 and TOP_K <= n_experts <= LANES
    rw = jnp.zeros((d, LANES), _BF16).at[:, :n_experts].set(router_w.astype(_BF16))
    rb = jnp.zeros((1, LANES), _F32).at[0, :n_experts].set(router_b)
    return pl.pallas_call(
        functools.partial(_router_kernel, n_experts=n_experts),
        grid=(t // tm,),
        in_specs=[pl.BlockSpec((tm, d), lambda i: (i, 0)), _const_spec((1, d)),
                  _const_spec((d, LANES)), _const_spec((1, LANES))],
        out_specs=pl.BlockSpec((tm, LANES), lambda i: (i, 0)),
        out_shape=jax.ShapeDtypeStruct((t, LANES), _F32),
        compiler_params=pltpu.CompilerParams(
            dimension_semantics=("arbitrary",), vmem_limit_bytes=VMEM_LIMIT_BYTES),
        name="router",
    )(x, norm_g, rw, rb)


def _route_plan(i1, i2, n_experts, tg):
    t = i1.shape[0]
    e_pair = jnp.concatenate([i1, i2])
    ids = jnp.arange(n_experts, dtype=jnp.int32)
    onehot = (e_pair[:, None] == ids[None, :]).astype(jnp.int32)
    csum = jnp.cumsum(onehot, axis=0)
    rank = jnp.sum(csum * onehot, axis=1) - 1
    counts = csum[-1]
    tiles_e = (counts + (tg - 1)) // tg
    tile_end = jnp.cumsum(tiles_e)
    start_row = (tile_end - tiles_e) * tg
    pos = jnp.sum(onehot * start_row[None, :], axis=1) + rank
    n_tiles = -(-2 * t // tg) + n_experts
    n_used = tile_end[-1]
    tile_ids = jnp.arange(n_tiles, dtype=jnp.int32)
    tile_e = jnp.sum((jnp.minimum(tile_ids, n_used - 1)[:, None] >= tile_end[None, :]).astype(jnp.int32), axis=1)
    pad_start = (start_row + counts).astype(jnp.int32)
    pad_len = (tiles_e * tg - counts).astype(jnp.int32)
    return pos.reshape(2, t), tile_e.astype(jnp.int32), n_used.reshape(1).astype(jnp.int32), pad_start, pad_len


def _row_copies(idx_ref, src_of, dst_of, sem, n, wait):
    def body(i, carry):
        for j in range(SUBLANES):
            for k in range(TOP_K):
                row = idx_ref[k * n + i * SUBLANES + j]
                cp = pltpu.make_async_copy(src_of(k, i, j, row), dst_of(k, i, j, row), sem)
                if wait:
                    cp.wait()
                else:
                    cp.start()
        return carry
    lax.fori_loop(0, n // SUBLANES, body, 0)


def _dispatch_kernel(pad_start_ref, pad_len_ref, pos_hbm, x_ref, xs_hbm, idx_ref, zero_ref, sem_idx, sem_rows,
                     *, tm):
    @pl.when(pl.program_id(0) == 0)
    def _zero_padding_rows():
        zero_ref[...] = jnp.zeros_like(zero_ref)
        for wait in (False, True):
            for e in range(pad_start_ref.shape[0]):
                def body(j, carry, e=e, wait=wait):
                    cp = pltpu.make_async_copy(zero_ref, xs_hbm.at[pl.ds(pad_start_ref[e] + j, 1)], sem_rows)
                    if wait:
                        cp.wait()
                    else:
                        cp.start()
                    return carry
                lax.fori_loop(0, pad_len_ref[e], body, 0)

    cp = pltpu.make_async_copy(pos_hbm.at[pl.program_id(0)], idx_ref, sem_idx)
    cp.start()
    cp.wait()
    src_of = lambda k, i, j, row: x_ref.at[i, pl.ds(j, 1)]
    dst_of = lambda k, i, j, row: xs_hbm.at[pl.ds(row, 1)]
    _row_copies(idx_ref, src_of, dst_of, sem_rows, tm, wait=False)
    _row_copies(idx_ref, src_of, dst_of, sem_rows, tm, wait=True)


def _dispatch(x, pos_tiles, pad_start, pad_len, n_rows, *, tm):
    t, d = x.shape
    assert tm % SUBLANES == 0
    return pl.pallas_call(
        functools.partial(_dispatch_kernel, tm=tm),
        grid_spec=pltpu.PrefetchScalarGridSpec(
            num_scalar_prefetch=2,
            grid=(t // tm,),
            in_specs=[pl.BlockSpec(memory_space=pl.ANY),
                      pl.BlockSpec((tm // SUBLANES, SUBLANES, d), lambda i, ps, pn: (i, 0, 0))],
            out_specs=pl.BlockSpec(memory_space=pl.ANY),
            scratch_shapes=[pltpu.SMEM((TOP_K * tm,), jnp.int32), pltpu.VMEM((1, d), _F32),
                            pltpu.SemaphoreType.DMA, pltpu.SemaphoreType.DMA]),
        out_shape=jax.ShapeDtypeStruct((n_rows, d), _F32),
        compiler_params=pltpu.CompilerParams(dimension_semantics=("arbitrary",)),
        name="dispatch",
    )(pad_start, pad_len, pos_tiles, x.reshape(t // SUBLANES, SUBLANES, d))


def _grouped_ffn_kernel(tile_e_ref, n_used_ref, xs_ref, ng_ref, wg_ref, wu_ref, wd_ref, ys_ref):
    del tile_e_ref

    @pl.when(pl.program_id(0) < n_used_ref[0])
    def _compute():
        h = _rms_norm(xs_ref[...], ng_ref[...])
        ys_ref[...] = _swiglu(h, wg_ref[0], wu_ref[0], wd_ref[0])


def _grouped_ffn(xs, norm_g, tile_e, n_used, wg, wu, wd, *, tg):
    n_rows, d = xs.shape
    _, _, ff = wg.shape
    row_map = lambda i, te, nu: (jnp.minimum(i, nu[0] - 1), 0)
    w_map = lambda i, te, nu: (te[i], 0, 0)
    w_spec = lambda shape: pl.BlockSpec(shape, w_map, pipeline_mode=pl.Buffered(1))
    return pl.pallas_call(
        _grouped_ffn_kernel,
        grid_spec=pltpu.PrefetchScalarGridSpec(
            num_scalar_prefetch=2,
            grid=(n_rows // tg,),
            in_specs=[pl.BlockSpec((tg, d), row_map),
                      pl.BlockSpec((1, d), lambda i, te, nu: (0, 0), pipeline_mode=pl.Buffered(1)),
                      w_spec((1, d, ff)), w_spec((1, d, ff)), w_spec((1, ff, d))],
            out_specs=pl.BlockSpec((tg, d), row_map)),
        out_shape=jax.ShapeDtypeStruct((n_rows, d), _F32),
        compiler_params=pltpu.CompilerParams(
            dimension_semantics=("arbitrary",), vmem_limit_bytes=VMEM_LIMIT_BYTES),
        name="grouped_ffn",
    )(tile_e, n_used, xs, norm_g, wg, wu, wd)


def _combine_kernel(pos_hbm, x_ref, route_ref, ys_hbm, g_ref, o_ref, idx_ref, buf_ref, sem_idx, sem_rows,
                    *, tm, final_norm):
    i = pl.program_id(0)
    cp = pltpu.make_async_copy(pos_hbm.at[i], idx_ref, sem_idx)
    cp.start()
    cp.wait()
    src_of = lambda k, i, j, row: ys_hbm.at[pl.ds(row, 1)]
    dst_of = lambda k, i, j, row: buf_ref.at[k, i, pl.ds(j, 1)]
    _row_copies(idx_ref, src_of, dst_of, sem_rows, tm, wait=False)
    _row_copies(idx_ref, src_of, dst_of, sem_rows, tm, wait=True)
    route = route_ref[...]
    y1 = buf_ref[0].reshape(x_ref.shape)
    y2 = buf_ref[1].reshape(x_ref.shape)
    out = x_ref[...] + route[:, 2:3] * y1 + route[:, 3:4] * y2
    if final_norm:
        out = _rms_norm(out, g_ref[...])
    o_ref[...] = out


def _combine(x, route, ys, pos_tiles, norm_g, *, tm, final_norm):
    t, d = x.shape
    return pl.pallas_call(
        functools.partial(_combine_kernel, tm=tm, final_norm=final_norm),
        grid=(t // tm,),
        in_specs=[pl.BlockSpec(memory_space=pl.ANY),
                  pl.BlockSpec((tm, d), lambda i: (i, 0)),
                  pl.BlockSpec((tm, LANES), lambda i: (i, 0)),
                  pl.BlockSpec(memory_space=pl.ANY),
                  _const_spec((1, d))],
        out_specs=pl.BlockSpec((tm, d), lambda i: (i, 0)),
        out_shape=jax.ShapeDtypeStruct((t, d), _F32),
        scratch_shapes=[pltpu.SMEM((TOP_K * tm,), jnp.int32),
                        pltpu.VMEM((TOP_K, tm // SUBLANES, SUBLANES, d), _F32),
                        pltpu.SemaphoreType.DMA, pltpu.SemaphoreType.DMA],
        compiler_params=pltpu.CompilerParams(
            dimension_semantics=("arbitrary",), vmem_limit_bytes=VMEM_LIMIT_BYTES),
        name="combine",
    )(pos_tiles, x, route, ys, norm_g)


def _moe_ffn(x, lw, norm_final, *, tm, tg, td, tc):
    t, d = x.shape
    n_experts = lw['router_w'].shape[1]
    route = _router(x, lw['norm_ffn'], lw['router_w'], lw['router_b'], tm=tm)
    i1 = route[:, 0].astype(jnp.int32)
    i2 = route[:, 1].astype(jnp.int32)
    pos, tile_e, n_used, pad_start, pad_len = _route_plan(i1, i2, n_experts, tg)
    pos_tiles = lambda n: pos.reshape(TOP_K, t // n, n).transpose(1, 0, 2).reshape(t // n, TOP_K * n)
    n_rows = tile_e.shape[0] * tg
    xs = _dispatch(x, pos_tiles(td), pad_start, pad_len, n_rows, tm=td)
    ys = _grouped_ffn(xs, lw['norm_ffn'], tile_e, n_used, lw['wg'], lw['wu'], lw['wd'], tg=tg)
    g = lw['norm_ffn'] if norm_final is None else norm_final
    return _combine(x, route, ys, pos_tiles(tc), g, tm=tc, final_norm=norm_final is not None)


def _final_norm_kernel(x_ref, g_ref, o_ref):
    o_ref[...] = _rms_norm(x_ref[...], g_ref[...])


def _final_norm(x, g, *, tm):
    t, d = x.shape
    assert t % tm == 0
    return pl.pallas_call(
        _final_norm_kernel,
        grid=(t // tm,),
        in_specs=[pl.BlockSpec((tm, d), lambda i: (i, 0)), _const_spec((1, d))],
        out_specs=pl.BlockSpec((tm, d), lambda i: (i, 0)),
        out_shape=jax.ShapeDtypeStruct((t, d), _F32),
        compiler_params=pltpu.CompilerParams(dimension_semantics=("arbitrary",)),
        name="final_norm",
    )(x, g)


def _trunk(x, conv_hist, pool_hist, pos0, layers, norm_final, *, tl, sb, tm, tg, td, tc):
    b, l, d = x.shape
    t = b * l
    tm, tg, td, tc = min(tm, t), min(tg, t), min(td, t), min(tc, t)
    new_conv, new_pool = [], []
    for li, lw in enumerate(layers):
        x, nc, npl = _mixer_layer(x, conv_hist[li], pool_hist[li], pos0, lw, tl=tl, sb=sb)
        new_conv.append(nc)
        new_pool.append(npl)
        xf = x.reshape(t, d)
        last = li == len(layers) - 1
        if 'router_w' in lw:
            xf = _moe_ffn(xf, lw, norm_final if last else None, tm=tm, tg=tg, td=td, tc=tc)
        else:
            xf = _dense_ffn(xf, lw['norm_ffn'], lw['wg'], lw['wu'], lw['wd'], tm=tm)
            if last:
                xf = _final_norm(xf, norm_final, tm=tm)
        x = xf.reshape(b, l, d)
    return x, jnp.stack(new_conv), jnp.stack(new_pool)


def kernel(x_prompt, x_sample, state_conv, state_pool, norm_mix, w_in, conv_w, conv_b, conv_ln_g, conv_ln_b, pool_w, pool_scale, w_out, norm_ffn, dense_w_gate, dense_w_up, dense_w_down, router_w, router_b, moe_w_gate, moe_w_up, moe_w_down, norm_final):
    depth = w_in.shape[0]
    row = lambda v: v.reshape(1, -1)
    layers = []
    for li in range(depth):
        lw = dict(
            norm_mix=row(norm_mix[li]), w_in=w_in[li].astype(_BF16),
            conv_w=conv_w[li].astype(_BF16).astype(_F32),
            conv_b=row(conv_b[li]), conv_ln_g=row(conv_ln_g[li]), conv_ln_b=row(conv_ln_b[li]),
            pool_w=pool_w[li].astype(_BF16), pool_scale=row(pool_scale[li]),
            w_out=w_out[li].astype(_BF16), norm_ffn=row(norm_ffn[li]))
        j = li // 2
        if li % 2 == 0:
            lw.update(wg=dense_w_gate[j].astype(_BF16), wu=dense_w_up[j].astype(_BF16),
                      wd=dense_w_down[j].astype(_BF16))
        else:
            lw.update(router_w=router_w[j], router_b=router_b[j], wg=moe_w_gate[j].astype(_BF16),
                      wu=moe_w_up[j].astype(_BF16), wd=moe_w_down[j].astype(_BF16))
        layers.append(lw)
    nf = row(norm_final)

    bp = x_prompt.shape[0]
    zero_conv = jnp.zeros((depth, bp) + state_conv.shape[2:], x_prompt.dtype)
    zero_pool = jnp.zeros((depth, bp) + state_pool.shape[2:], x_prompt.dtype)
    y_p, conv_p, pool_p = _trunk(x_prompt, zero_conv, zero_pool, 0, layers, nf, **TILES)
    y_s, conv_s, pool_s = _trunk(x_sample, state_conv, state_pool, PAST_LEN, layers, nf, **TILES)
    return (y_p, y_s, conv_p, pool_p, conv_s, pool_s)
```

```python
import functools

import jax
import jax.numpy as jnp
from jax import lax
from jax.experimental import pallas as pl
from jax.experimental.pallas import tpu as pltpu

POOL_WINDOWS = (2, 4, 8, 16)
PAST_LEN = 2048
TOP_K = 2
EPS = 1e-6
LANES = 128
SUBLANES = 8
VMEM_LIMIT_BYTES = 56 * 1024 * 1024
TILES = dict(tl=512, sb=128, tm=512, tg=512, td=4096, tc=1024)

_F32 = jnp.float32
_BF16 = jnp.bfloat16


def _round_up(n, m):
    return (n + m - 1) // m * m


def _rms_norm(x, g):
    ms = jnp.mean(x * x, axis=-1, keepdims=True)
    return x * lax.rsqrt(ms + EPS) * g


def _dot_f32_lhs(h, w):
    return lax.dot_general(h, w, (((1,), (0,)), ((), ())), preferred_element_type=_F32)


def _const_spec(shape):
    nd = len(shape)
    return pl.BlockSpec(shape, lambda *_: (0,) * nd, pipeline_mode=pl.Buffered(1))


def _mixer_kernel(x_ref, ch_ref, ph_ref, ng_ref, win_ref, cw_ref, cb_ref, lg_ref, lb_ref,
                  pw_ref, ps_ref, wout_ref,
                  y_ref, nc_ref, np_ref,
                  *scratch, pos0, tl, sb, rb, n_t):
    n_sb = tl // sb
    exta, extp = scratch[:n_sb], scratch[n_sb:2 * n_sb]
    h_ref, mix_ref, d_ref = scratch[2 * n_sb:]
    t = pl.program_id(1)
    n_cc = exta[0].shape[0]
    n_g = extp[0].shape[0]
    cc = n_cc * LANES
    k_taps = cw_ref.shape[0]
    kh = k_taps - 1
    ha = exta[0].shape[1] - sb
    ph = ph_ref.shape[1]
    hp = extp[0].shape[1] - sb

    def copy_history(dst_a, dst_p, src_a, src_p):
        for j in range(n_cc):
            dst_a[j, ha - kh:ha, :] = src_a[j, ha + sb - kh:ha + sb, :]
        for g in range(n_g):
            dst_p[g, hp - ph:hp, :] = src_p[g, hp + sb - ph:hp + sb, :]

    as_operand = lambda v: v.astype(_BF16).astype(_F32)

    @pl.when(t == 0)
    def _load_state():
        for j in range(n_cc):
            exta[0][j, ha - kh:ha, :] = as_operand(ch_ref[0, :, j * LANES:(j + 1) * LANES])
        for g in range(n_g):
            extp[0][g, hp - ph:hp, :] = ph_ref[0, :, g * LANES:(g + 1) * LANES]

    def project_in(s):
        ea, ep = exta[s], extp[s]
        if s > 0:
            copy_history(ea, ep, exta[s - 1], extp[s - 1])
        u = _dot_f32_lhs(h_ref[s * sb:(s + 1) * sb, :], win_ref[...])
        a = u[:, :cc] * jax.nn.sigmoid(u[:, cc:2 * cc])
        for j in range(n_cc):
            ea[j, ha:ha + sb, :] = as_operand(a[:, j * LANES:(j + 1) * LANES])
        for g in range(n_g):
            ep[g, hp:hp + sb, :] = u[:, 2 * cc + g * LANES:2 * cc + (g + 1) * LANES]
        if s == n_sb - 1:
            if sb >= kh:
                nc_ref[0] = a[sb - kh:, :]
            else:
                nc_ref[0, :kh - sb, :] = ch_ref[0, sb:, :]
                nc_ref[0, kh - sb:, :] = a

    def mix_block(s):
        s0 = s * sb
        rows = slice(s0, s0 + sb)
        ea, ep = exta[s], extp[s]
        for r0 in range(0, sb, rb):
            ys = []
            for j in range(n_cc):
                ls = slice(j * LANES, (j + 1) * LANES)
                acc = jnp.zeros((rb, LANES), _F32)
                for k in range(k_taps):
                    e0 = r0 + ha - kh + k
                    acc = acc + cw_ref[k:k + 1, ls] * ea[j, e0:e0 + rb, :]
                ys.append(acc + cb_ref[:, ls])
            tot = ys[0]
            for y in ys[1:]:
                tot = tot + y
            mu = jnp.sum(tot, axis=-1, keepdims=True) * (1.0 / cc)
            xc = [y - mu for y in ys]
            sq = xc[0] * xc[0]
            for c in xc[1:]:
                sq = sq + c * c
            inv = lax.rsqrt(jnp.sum(sq, axis=-1, keepdims=True) * (1.0 / cc) + EPS)
            for j in range(n_cc):
                ls = slice(j * LANES, (j + 1) * LANES)
                v = xc[j] * inv * lg_ref[:, ls] + lb_ref[:, ls]
                mix_ref[s0 + r0:s0 + r0 + rb, ls] = (v * jax.nn.sigmoid(v)).astype(_BF16)
            row = lax.broadcasted_iota(jnp.int32, (rb, 1), 0)
            pos1 = (row + (pos0 + 1 + t * tl + s0 + r0)).astype(_F32)
            for g, w in enumerate(POOL_WINDOWS):
                cur = ep[g, hp + r0:hp + r0 + rb, :]
                acc = cur
                for back in range(1, w):
                    acc = acc + ep[g, hp + r0 - back:hp + r0 - back + rb, :]
                d = acc / jnp.minimum(pos1, float(w)) - cur
                d_ref[s0 + r0:s0 + r0 + rb, g * LANES:(g + 1) * LANES] = d.astype(_BF16)

        for g in range(n_g):
            ls = slice(g * LANES, (g + 1) * LANES)
            z = jnp.dot(d_ref[rows, ls], pw_ref[g], preferred_element_type=_F32) * ps_ref[:, ls]
            mix_ref[rows, cc + g * LANES:cc + (g + 1) * LANES] = z.astype(_BF16)
        y_ref[0, rows, :] = x_ref[0, rows, :] + jnp.dot(mix_ref[rows, :], wout_ref[...],
                                                        preferred_element_type=_F32)

    h_ref[...] = _rms_norm(x_ref[0], ng_ref[...])
    project_in(0)
    for s in range(n_sb):
        if s + 1 < n_sb:
            project_in(s + 1)
        mix_block(s)

    last_a, last_p = exta[n_sb - 1], extp[n_sb - 1]
    for g in range(n_g):
        np_ref[0, :, g * LANES:(g + 1) * LANES] = last_p[g, hp + sb - ph:hp + sb, :]
    if n_t > 1:
        copy_history(exta[0], extp[0], last_a, last_p)


def _mixer_layer(x, conv_hist, pool_hist, pos0, w, *, tl, sb):
    b, l, d = x.shape
    kh, cc = conv_hist.shape[1], conv_hist.shape[2]
    ph, pc = pool_hist.shape[1], pool_hist.shape[2]
    n_g = len(POOL_WINDOWS)
    assert cc % LANES == 0 and pc == n_g * LANES, (cc, pc)
    assert ph >= max(POOL_WINDOWS) - 1 and w['conv_w'].shape[0] == kh + 1
    tl = min(tl, l)
    n_t = l // tl
    sb = min(sb, tl)
    assert l % tl == 0 and (l == sb or sb >= max(kh, ph))
    rb = min(16, sb)
    assert tl % sb == 0 and sb % rb == 0 and rb % 16 == 0
    ha, hp = _round_up(kh, SUBLANES), _round_up(ph, SUBLANES)

    kern = functools.partial(_mixer_kernel, pos0=pos0, tl=tl, sb=sb, rb=rb, n_t=n_t)
    seq_spec = lambda rows, ch: pl.BlockSpec((1, rows, ch), lambda i, j: (i, 0, 0))
    return pl.pallas_call(
        kern,
        grid=(b, n_t),
        in_specs=[
            pl.BlockSpec((1, tl, d), lambda i, j: (i, j, 0)),
            seq_spec(kh, cc), seq_spec(ph, pc),
            _const_spec((1, d)), _const_spec(w['w_in'].shape),
            _const_spec(w['conv_w'].shape), _const_spec((1, cc)), _const_spec((1, cc)), _const_spec((1, cc)),
            _const_spec(w['pool_w'].shape), _const_spec((1, pc)), _const_spec(w['w_out'].shape),
        ],
        out_specs=[
            pl.BlockSpec((1, tl, d), lambda i, j: (i, j, 0)),
            seq_spec(kh, cc), seq_spec(ph, pc),
        ],
        out_shape=[
            jax.ShapeDtypeStruct((b, l, d), _F32),
            jax.ShapeDtypeStruct((b, kh, cc), _F32),
            jax.ShapeDtypeStruct((b, ph, pc), _F32),
        ],
        scratch_shapes=(
            [pltpu.VMEM((cc // LANES, ha + sb, LANES), _F32)] * (tl // sb)
            + [pltpu.VMEM((n_g, hp + sb, LANES), _F32)] * (tl // sb)
            + [pltpu.VMEM((tl, d), _F32), pltpu.VMEM((tl, cc + pc), _BF16), pltpu.VMEM((tl, pc), _BF16)]),
        compiler_params=pltpu.CompilerParams(
            dimension_semantics=("arbitrary", "arbitrary"), vmem_limit_bytes=VMEM_LIMIT_BYTES),
        name="mixer",
    )(x, conv_hist, pool_hist, w['norm_mix'], w['w_in'], w['conv_w'], w['conv_b'], w['conv_ln_g'],
      w['conv_ln_b'], w['pool_w'], w['pool_scale'], w['w_out'])


def _swiglu(h, wg, wu, wd):
    g = _dot_f32_lhs(h, wg)
    u = _dot_f32_lhs(h, wu)
    act = (g * jax.nn.sigmoid(g) * u).astype(_BF16)
    return jnp.dot(act, wd, preferred_element_type=_F32)


def _dense_ffn_kernel(x_ref, ng_ref, wg_ref, wu_ref, wd_ref, o_ref):
    x = x_ref[...]
    h = _rms_norm(x, ng_ref[...])
    o_ref[...] = x + _swiglu(h, wg_ref[...], wu_ref[...], wd_ref[...])


def _dense_ffn(x, norm_g, wg, wu, wd, *, tm):
    t, d = x.shape
    assert t % tm == 0
    return pl.pallas_call(
        _dense_ffn_kernel,
        grid=(t // tm,),
        in_specs=[pl.BlockSpec((tm, d), lambda i: (i, 0)), _const_spec((1, d)),
                  _const_spec(wg.shape), _const_spec(wu.shape), _const_spec(wd.shape)],
        out_specs=pl.BlockSpec((tm, d), lambda i: (i, 0)),
        out_shape=jax.ShapeDtypeStruct((t, d), _F32),
        compiler_params=pltpu.CompilerParams(
            dimension_semantics=("arbitrary",), vmem_limit_bytes=VMEM_LIMIT_BYTES),
        name="dense_ffn",
    )(x, norm_g, wg, wu, wd)


def _router_kernel(x_ref, ng_ref, rw_ref, rb_ref, route_ref, *, n_experts):
    h = _rms_norm(x_ref[...], ng_ref[...])
    logits = _dot_f32_lhs(h, rw_ref[...]) + rb_ref[...]
    lane = lax.broadcasted_iota(jnp.int32, logits.shape, 1)
    neg = jnp.float32(-jnp.inf)
    logits = jnp.where(lane < n_experts, logits, neg)
    m1 = jnp.max(logits, axis=-1, keepdims=True)
    i1 = jnp.min(jnp.where(logits == m1, lane, LANES), axis=-1, keepdims=True)
    rest = jnp.where(lane == i1, neg, logits)
    m2 = jnp.max(rest, axis=-1, keepdims=True)
    i2 = jnp.min(jnp.where(rest == m2, lane, LANES), axis=-1, keepdims=True)
    e2 = jnp.exp(m2 - m1)
    den = 1.0 + e2
    route = jnp.where(lane == 0, i1.astype(_F32), 0.0)
    route = jnp.where(lane == 1, i2.astype(_F32), route)
    route = jnp.where(lane == 2, 1.0 / den, route)
    route_ref[...] = jnp.where(lane == 3, e2 / den, route)


def _router(x, norm_g, router_w, router_b, *, tm):
    t, d = x.shape
    n_experts = router_w.shape[1]
    assert t % tm == 0 and TOP_K <= n_experts <= LANES
    rw = jnp.zeros((d, LANES), _BF16).at[:, :n_experts].set(router_w.astype(_BF16))
    rb = jnp.zeros((1, LANES), _F32).at[0, :n_experts].set(router_b)
    return pl.pallas_call(
        functools.partial(_router_kernel, n_experts=n_experts),
        grid=(t // tm,),
        in_specs=[pl.BlockSpec((tm, d), lambda i: (i, 0)), _const_spec((1, d)),
                  _const_spec((d, LANES)), _const_spec((1, LANES))],
        out_specs=pl.BlockSpec((tm, LANES), lambda i: (i, 0)),
        out_shape=jax.ShapeDtypeStruct((t, LANES), _F32),
        compiler_params=pltpu.CompilerParams(
            dimension_semantics=("arbitrary",), vmem_limit_bytes=VMEM_LIMIT_BYTES),
        name="router",
    )(x, norm_g, rw, rb)


def _route_plan(i1, i2, n_experts, tg):
    t = i1.shape[0]
    e_pair = jnp.concatenate([i1, i2])
    ids = jnp.arange(n_experts, dtype=jnp.int32)
    onehot = (e_pair[:, None] == ids[None, :]).astype(jnp.int32)
    csum = jnp.cumsum(onehot, axis=0)
    rank = jnp.sum(csum * onehot, axis=1) - 1
    counts = csum[-1]
    tiles_e = (counts + (tg - 1)) // tg
    tile_end = jnp.cumsum(tiles_e)
    start_row = (tile_end - tiles_e) * tg
    pos = jnp.sum(onehot * start_row[None, :], axis=1) + rank
    n_tiles = -(-2 * t // tg) + n_experts
    n_used = tile_end[-1]
    tile_ids = jnp.arange(n_tiles, dtype=jnp.int32)
    tile_e = jnp.sum((jnp.minimum(tile_ids, n_used - 1)[:, None] >= tile_end[None, :]).astype(jnp.int32), axis=1)
    pad_start = (start_row + counts).astype(jnp.int32)
    pad_len = (tiles_e * tg - counts).astype(jnp.int32)
    return pos.reshape(2, t), tile_e.astype(jnp.int32), n_used.reshape(1).astype(jnp.int32), pad_start, pad_len


def _row_copies(idx_ref, src_of, dst_of, sem, n, wait):
    def body(i, carry):
        for j in range(SUBLANES):
            for k in range(TOP_K):
                row = idx_ref[k * n + i * SUBLANES + j]
                cp = pltpu.make_async_copy(src_of(k, i, j, row), dst_of(k, i, j, row), sem)
                if wait:
                    cp.wait()
                else:
                    cp.start()
        return carry
    lax.fori_loop(0, n // SUBLANES, body, 0)


def _dispatch_kernel(pad_start_ref, pad_len_ref, pos_hbm, x_ref, xs_hbm, idx_ref, zero_ref, sem_idx, sem_rows,
                     *, tm):
    @pl.when(pl.program_id(0) == 0)
    def _zero_padding_rows():
        zero_ref[...] = jnp.zeros_like(zero_ref)
        for wait in (False, True):
            for e in range(pad_start_ref.shape[0]):
                def body(j, carry, e=e, wait=wait):
                    cp = pltpu.make_async_copy(zero_ref, xs_hbm.at[pl.ds(pad_start_ref[e] + j, 1)], sem_rows)
                    if wait:
                        cp.wait()
                    else:
                        cp.start()
                    return carry
                lax.fori_loop(0, pad_len_ref[e], body, 0)

    cp = pltpu.make_async_copy(pos_hbm.at[pl.program_id(0)], idx_ref, sem_idx)
    cp.start()
    cp.wait()
    src_of = lambda k, i, j, row: x_ref.at[i, pl.ds(j, 1)]
    dst_of = lambda k, i, j, row: xs_hbm.at[pl.ds(row, 1)]
    _row_copies(idx_ref, src_of, dst_of, sem_rows, tm, wait=False)
    _row_copies(idx_ref, src_of, dst_of, sem_rows, tm, wait=True)


def _dispatch(x, pos_tiles, pad_start, pad_len, n_rows, *, tm):
    t, d = x.shape
    assert tm % SUBLANES == 0
    return pl.pallas_call(
        functools.partial(_dispatch_kernel, tm=tm),
        grid_spec=pltpu.PrefetchScalarGridSpec(
            num_scalar_prefetch=2,
            grid=(t // tm,),
            in_specs=[pl.BlockSpec(memory_space=pl.ANY),
                      pl.BlockSpec((tm // SUBLANES, SUBLANES, d), lambda i, ps, pn: (i, 0, 0))],
            out_specs=pl.BlockSpec(memory_space=pl.ANY),
            scratch_shapes=[pltpu.SMEM((TOP_K * tm,), jnp.int32), pltpu.VMEM((1, d), _F32),
                            pltpu.SemaphoreType.DMA, pltpu.SemaphoreType.DMA]),
        out_shape=jax.ShapeDtypeStruct((n_rows, d), _F32),
        compiler_params=pltpu.CompilerParams(dimension_semantics=("arbitrary",)),
        name="dispatch",
    )(pad_start, pad_len, pos_tiles, x.reshape(t // SUBLANES, SUBLANES, d))


def _grouped_ffn_kernel(tile_e_ref, n_used_ref, xs_ref, ng_ref, wg_ref, wu_ref, wd_ref, ys_ref):
    del tile_e_ref

    @pl.when(pl.program_id(0) < n_used_ref[0])
    def _compute():
        h = _rms_norm(xs_ref[...], ng_ref[...])
        ys_ref[...] = _swiglu(h, wg_ref[...], wu_ref[...], wd_ref[...])


def _grouped_ffn(xs, norm_g, tile_e, n_used, wg, wu, wd, layer, *, tg):
    n_rows, d = xs.shape
    ff = wg.shape[-1]
    row_map = lambda i, te, nu: (jnp.minimum(i, nu[0] - 1), 0)
    w_map = lambda i, te, nu: (layer, te[i], 0, 0)
    w_spec = lambda rows, cols: pl.BlockSpec((None, None, rows, cols), w_map, pipeline_mode=pl.Buffered(1))
    return pl.pallas_call(
        _grouped_ffn_kernel,
        grid_spec=pltpu.PrefetchScalarGridSpec(
            num_scalar_prefetch=2,
            grid=(n_rows // tg,),
            in_specs=[pl.BlockSpec((tg, d), row_map),
                      pl.BlockSpec((1, d), lambda i, te, nu: (0, 0), pipeline_mode=pl.Buffered(1)),
                      w_spec(d, ff), w_spec(d, ff), w_spec(ff, d)],
            out_specs=pl.BlockSpec((tg, d), row_map)),
        out_shape=jax.ShapeDtypeStruct((n_rows, d), _F32),
        compiler_params=pltpu.CompilerParams(
            dimension_semantics=("arbitrary",), vmem_limit_bytes=VMEM_LIMIT_BYTES),
        name="grouped_ffn",
    )(tile_e, n_used, xs, norm_g, wg, wu, wd)


def _combine_kernel(pos_hbm, x_ref, route_ref, ys_hbm, g_ref, o_ref, idx_ref, buf_ref, sem_idx, sem_rows,
                    *, tm, final_norm):
    i = pl.program_id(0)
    cp = pltpu.make_async_copy(pos_hbm.at[i], idx_ref, sem_idx)
    cp.start()
    cp.wait()
    src_of = lambda k, i, j, row: ys_hbm.at[pl.ds(row, 1)]
    dst_of = lambda k, i, j, row: buf_ref.at[k, i, pl.ds(j, 1)]
    _row_copies(idx_ref, src_of, dst_of, sem_rows, tm, wait=False)
    _row_copies(idx_ref, src_of, dst_of, sem_rows, tm, wait=True)
    route = route_ref[...]
    y1 = buf_ref[0].reshape(x_ref.shape)
    y2 = buf_ref[1].reshape(x_ref.shape)
    out = x_ref[...] + route[:, 2:3] * y1 + route[:, 3:4] * y2
    if final_norm:
        out = _rms_norm(out, g_ref[...])
    o_ref[...] = out


def _combine(x, route, ys, pos_tiles, norm_g, *, tm, final_norm):
    t, d = x.shape
    return pl.pallas_call(
        functools.partial(_combine_kernel, tm=tm, final_norm=final_norm),
        grid=(t // tm,),
        in_specs=[pl.BlockSpec(memory_space=pl.ANY),
                  pl.BlockSpec((tm, d), lambda i: (i, 0)),
                  pl.BlockSpec((tm, LANES), lambda i: (i, 0)),
                  pl.BlockSpec(memory_space=pl.ANY),
                  _const_spec((1, d))],
        out_specs=pl.BlockSpec((tm, d), lambda i: (i, 0)),
        out_shape=jax.ShapeDtypeStruct((t, d), _F32),
        scratch_shapes=[pltpu.SMEM((TOP_K * tm,), jnp.int32),
                        pltpu.VMEM((TOP_K, tm // SUBLANES, SUBLANES, d), _F32),
                        pltpu.SemaphoreType.DMA, pltpu.SemaphoreType.DMA],
        compiler_params=pltpu.CompilerParams(
            dimension_semantics=("arbitrary",), vmem_limit_bytes=VMEM_LIMIT_BYTES),
        name="combine",
    )(pos_tiles, x, route, ys, norm_g)


def _moe_ffn(x, lw, norm_final, *, tm, tg, td, tc):
    t, d = x.shape
    n_experts = lw['router_w'].shape[1]
    route = _router(x, lw['norm_ffn'], lw['router_w'], lw['router_b'], tm=tm)
    i1 = route[:, 0].astype(jnp.int32)
    i2 = route[:, 1].astype(jnp.int32)
    pos, tile_e, n_used, pad_start, pad_len = _route_plan(i1, i2, n_experts, tg)
    pos_tiles = lambda n: pos.reshape(TOP_K, t // n, n).transpose(1, 0, 2).reshape(t // n, TOP_K * n)
    n_rows = tile_e.shape[0] * tg
    xs = _dispatch(x, pos_tiles(td), pad_start, pad_len, n_rows, tm=td)
    ys = _grouped_ffn(xs, lw['norm_ffn'], tile_e, n_used, lw['wg'], lw['wu'], lw['wd'], lw['moe_layer'], tg=tg)
    g = lw['norm_ffn'] if norm_final is None else norm_final
    return _combine(x, route, ys, pos_tiles(tc), g, tm=tc, final_norm=norm_final is not None)


def _final_norm_kernel(x_ref, g_ref, o_ref):
    o_ref[...] = _rms_norm(x_ref[...], g_ref[...])


def _final_norm(x, g, *, tm):
    t, d = x.shape
    assert t % tm == 0
    return pl.pallas_call(
        _final_norm_kernel,
        grid=(t // tm,),
        in_specs=[pl.BlockSpec((tm, d), lambda i: (i, 0)), _const_spec((1, d))],
        out_specs=pl.BlockSpec((tm, d), lambda i: (i, 0)),
        out_shape=jax.ShapeDtypeStruct((t, d), _F32),
        compiler_params=pltpu.CompilerParams(dimension_semantics=("arbitrary",)),
        name="final_norm",
    )(x, g)


def _trunk(x, conv_hist, pool_hist, pos0, layers, norm_final, *, tl, sb, tm, tg, td, tc):
    b, l, d = x.shape
    t = b * l
    tm, tg, td, tc = min(tm, t), min(tg, t), min(td, t), min(tc, t)
    new_conv, new_pool = [], []
    for li, lw in enumerate(layers):
        x, nc, npl = _mixer_layer(x, conv_hist[li], pool_hist[li], pos0, lw, tl=tl, sb=sb)
        new_conv.append(nc)
        new_pool.append(npl)
        xf = x.reshape(t, d)
        last = li == len(layers) - 1
        if 'router_w' in lw:
            xf = _moe_ffn(xf, lw, norm_final if last else None, tm=tm, tg=tg, td=td, tc=tc)
        else:
            xf = _dense_ffn(xf, lw['norm_ffn'], lw['wg'], lw['wu'], lw['wd'], tm=tm)
            if last:
                xf = _final_norm(xf, norm_final, tm=tm)
        x = xf.reshape(b, l, d)
    return x, jnp.stack(new_conv), jnp.stack(new_pool)


def kernel(x_prompt, x_sample, state_conv, state_pool, norm_mix, w_in, conv_w, conv_b, conv_ln_g, conv_ln_b, pool_w, pool_scale, w_out, norm_ffn, dense_w_gate, dense_w_up, dense_w_down, router_w, router_b, moe_w_gate, moe_w_up, moe_w_down, norm_final):
    depth = w_in.shape[0]
    row = lambda v: v.reshape(1, -1)
    moe_wg, moe_wu, moe_wd = (w.astype(_BF16) for w in (moe_w_gate, moe_w_up, moe_w_down))
    layers = []
    for li in range(depth):
        lw = dict(
            norm_mix=row(norm_mix[li]), w_in=w_in[li].astype(_BF16),
            conv_w=conv_w[li].astype(_BF16).astype(_F32),
            conv_b=row(conv_b[li]), conv_ln_g=row(conv_ln_g[li]), conv_ln_b=row(conv_ln_b[li]),
            pool_w=pool_w[li].astype(_BF16), pool_scale=row(pool_scale[li]),
            w_out=w_out[li].astype(_BF16), norm_ffn=row(norm_ffn[li]))
        j = li // 2
        if li % 2 == 0:
            lw.update(wg=dense_w_gate[j].astype(_BF16), wu=dense_w_up[j].astype(_BF16),
                      wd=dense_w_down[j].astype(_BF16))
        else:
            lw.update(router_w=router_w[j], router_b=router_b[j], wg=moe_wg, wu=moe_wu, wd=moe_wd,
                      moe_layer=j)
        layers.append(lw)
    nf = row(norm_final)

    bp = x_prompt.shape[0]
    zero_conv = jnp.zeros((depth, bp) + state_conv.shape[2:], x_prompt.dtype)
    zero_pool = jnp.zeros((depth, bp) + state_pool.shape[2:], x_prompt.dtype)
    y_p, conv_p, pool_p = _trunk(x_prompt, zero_conv, zero_pool, 0, layers, nf, **TILES)
    y_s, conv_s, pool_s = _trunk(x_sample, state_conv, state_pool, PAST_LEN, layers, nf, **TILES)
    return (y_p, y_s, conv_p, pool_p, conv_s, pool_s)
```
